```python
import math
import jax
import jax.numpy as jnp
from jax import lax
import numpy as np

D_MODEL = 4096
BATCH = 4
SEQ = 2048
DEPTH = 1
DEC_BATCH = 128
DEC_SEQ = 8
PAST_LEN = 2048
PAGE_SIZE = 128

HEAD_DIM = 128
NSA_HEADS = D_MODEL // (2 * HEAD_DIM)
NSA_GROUPS = NSA_HEADS // 4
NSA_REP = NSA_HEADS // NSA_GROUPS
CMP_BLOCK = 32
CMP_STRIDE = 16
CMP_HIDDEN = 2 * HEAD_DIM
SEL_BLOCK = 64
N_SELECT = 16
WINDOW = 512
DIFF_HEADS = D_MODEL // (4 * HEAD_DIM)
DIFF_VDIM = 2 * HEAD_DIM
ROPE_THETA = 10000.0
NORM_EPS = 1e-6
Q_BLOCK = 128
SEL_Q_BLOCK = 32
FORCE_SCORE = 1e4
NEG_INF = -1e30
SCALE = HEAD_DIM ** -0.5

NSA_Q_W = NSA_HEADS * HEAD_DIM
NSA_KV_W = NSA_GROUPS * HEAD_DIM
DIFF_W = DIFF_HEADS * DIFF_VDIM
IN_SPLITS = (NSA_Q_W, 6 * NSA_KV_W, 3 * NSA_HEADS, NSA_Q_W, DIFF_W, DIFF_W, DIFF_W, DIFF_W, D_MODEL, D_MODEL)
IN_COLS = sum(IN_SPLITS)

kernel_name = 'nsa_diffattn_gated_hybrid_step'


def rms_norm(x, g):
    xf = x.astype(jnp.float32)
    xf = xf * lax.rsqrt(jnp.mean(xf * xf, axis=-1, keepdims=True) + NORM_EPS)
    return (xf * g.astype(jnp.float32)).astype(x.dtype)


def rope(x, pos):
    half = x.shape[-1] // 2
    inv = ROPE_THETA ** (-jnp.arange(half, dtype=jnp.float32) / half)
    ang = pos.astype(jnp.float32)[:, None] * inv[None, :]
    shape = (1, x.shape[1]) + (1,) * (x.ndim - 3) + (half,)
    cos = jnp.cos(ang).reshape(shape)
    sin = jnp.sin(ang).reshape(shape)
    xf = x.astype(jnp.float32)
    x1, x2 = xf[..., :half], xf[..., half:]
    return jnp.concatenate([x1 * cos - x2 * sin, x1 * sin + x2 * cos], axis=-1).astype(x.dtype)


def project(x, norm_g, w_in, pos):
    n, s, _ = x.shape
    h = rms_norm(x, norm_g)
    cuts = [sum(IN_SPLITS[:i + 1]) for i in range(len(IN_SPLITS) - 1)]
    q, kv, gates, z_nsa, dq, dk, dv, z_diff, g_a, g_b = jnp.split(h @ w_in, cuts, axis=-1)
    kv = kv.reshape(n, s, 6, NSA_GROUPS, HEAD_DIM)
    return dict(
        q=rope(q.reshape(n, s, NSA_GROUPS, NSA_REP, HEAD_DIM), pos),
        k_cmp=rope(kv[:, :, 0], pos), v_cmp=kv[:, :, 1],
        k_sel=rope(kv[:, :, 2], pos), v_sel=kv[:, :, 3],
        k_win=rope(kv[:, :, 4], pos), v_win=kv[:, :, 5],
        gates=jax.nn.sigmoid(gates).reshape(n, s, NSA_GROUPS, NSA_REP, 3),
        z_nsa=z_nsa,
        dq=rope(dq.reshape(n, s, DIFF_HEADS, 2, HEAD_DIM), pos),
        dk=rope(dk.reshape(n, s, DIFF_HEADS, 2, HEAD_DIM), pos),
        dv=dv.reshape(n, s, DIFF_HEADS, DIFF_VDIM),
        z_diff=z_diff, g_a=g_a, g_b=g_b)


def compress(rows, w1, pe, w2):
    n, t = rows.shape[:2]
    nh = t // CMP_STRIDE
    halves = rows[:, :nh * CMP_STRIDE].reshape(n, nh, CMP_STRIDE, NSA_GROUPS, HEAD_DIM)
    w1b = w1.reshape(CMP_BLOCK // CMP_STRIDE, CMP_STRIDE, HEAD_DIM, CMP_HIDDEN)
    first = jnp.einsum('nhsgd,sdk->nhgk', halves, w1b[0])
    second = jnp.einsum('nhsgd,sdk->nhgk', halves, w1b[1])
    pos_term = jnp.einsum('ld,ldk->k', pe, w1.reshape(CMP_BLOCK, HEAD_DIM, CMP_HIDDEN))
    pre = first[:, :-1] + second[:, 1:] + pos_term
    return jax.nn.silu(pre) @ w2


def cmp_attend(q, kc, vc, pos):
    nc = kc.shape[1]
    end = jnp.arange(nc) * CMP_STRIDE + CMP_BLOCK
    vmask = (end[None, :] <= pos[:, None] + 1)[None, :, None, None, :]
    s = jnp.einsum('nsgrd,ncgd->nsgrc', q, kc).astype(jnp.float32) * SCALE
    p = jax.nn.softmax(jnp.where(vmask, s, NEG_INF), axis=-1) * vmask
    o = jnp.einsum('nsgrc,ncgd->nsgrd', p.astype(vc.dtype), vc)
    return o, p


def block_scores(p, n_blocks):
    nc = p.shape[-1]
    c_start = jnp.arange(nc) * CMP_STRIDE
    b_start = jnp.arange(n_blocks) * SEL_BLOCK
    cover = ((c_start[:, None] < b_start[None, :] + SEL_BLOCK)
             & (c_start[:, None] + CMP_BLOCK > b_start[None, :])).astype(p.dtype)
    return jnp.einsum('nsgrc,cj->nsgj', p, cover)


def select_prompt(score, pos):
    nb = score.shape[-1]
    j = jnp.arange(nb)
    cur = (pos // SEL_BLOCK)[:, None, None]
    forced = (j == 0) | (j == cur) | (j == cur - 1)
    ranked = jnp.where(forced, FORCE_SCORE, jnp.where(j <= cur, score, -1.0))
    _, idx = lax.top_k(ranked, min(N_SELECT, nb))
    return idx, idx <= cur


def sel_attend_prompt(q, k, v, idx, valid, pos):
    n, t = q.shape[:2]
    nb = t // SEL_BLOCK
    kb = k.reshape(n, nb, SEL_BLOCK, NSA_GROUPS, HEAD_DIM)
    vb = v.reshape(n, nb, SEL_BLOCK, NSA_GROUPS, HEAD_DIM)
    nc = t // SEL_Q_BLOCK
    bi = jnp.arange(n)[:, None, None, None]
    gi = jnp.arange(NSA_GROUPS)[None, None, :, None]

    def to_chunks(a):
        return jnp.moveaxis(a.reshape((n, nc, SEL_Q_BLOCK) + a.shape[2:]), 1, 0)

    def chunk(args):
        qc, ic, okc, pc = args
        kg = kb[bi, ic, :, gi]
        vg = vb[bi, ic, :, gi]
        kpos = ic[..., None] * SEL_BLOCK + jnp.arange(SEL_BLOCK)
        mask = okc[..., None] & (kpos <= pc[None, :, None, None, None])
        s = jnp.einsum('ncgrd,ncgkbd->ncgrkb', qc, kg).astype(jnp.float32) * SCALE
        s = jnp.where(mask[:, :, :, None], s, NEG_INF)
        shp = s.shape
        p = jax.nn.softmax(s.reshape(shp[:4] + (-1,)), axis=-1).reshape(shp)
        return jnp.einsum('ncgrkb,ncgkbd->ncgrd', p.astype(vg.dtype), vg)

    out = lax.map(chunk, (to_chunks(q), to_chunks(idx), to_chunks(valid), pos.reshape(nc, SEL_Q_BLOCK)))
    return jnp.moveaxis(out, 0, 1).reshape(q.shape)


def select_sample(score):
    nbp = score.shape[-1]
    j = jnp.arange(nbp)
    ranked = jnp.where((j == 0) | (j == nbp - 1), FORCE_SCORE, score)
    _, idx = lax.top_k(ranked, min(N_SELECT - 1, nbp))
    return idx


def sel_attend_sample(q, k_new, v_new, idx, pool_k, pool_v, layer, page_table):
    n, s = q.shape[:2]
    bpp = PAGE_SIZE // SEL_BLOCK
    pk = pool_k.reshape((pool_k.shape[0], -1, SEL_BLOCK) + pool_k.shape[3:])
    pv = pool_v.reshape((pool_v.shape[0], -1, SEL_BLOCK) + pool_v.shape[3:])
    page = jnp.take_along_axis(page_table, (idx // bpp).reshape(n, -1), axis=1).reshape(idx.shape)
    phys = page * bpp + idx % bpp
    li = jnp.full((1, 1, 1), layer, jnp.int32)
    gi = jnp.arange(NSA_GROUPS)[None, :, None]
    jpos = jnp.arange(s)

    def step(args):
        qt, pt, t = args
        kg = pk[li, pt, :, gi]
        vg = pv[li, pt, :, gi]
        sp = jnp.einsum('ngrd,ngkbd->ngrkb', qt, kg).astype(jnp.float32) * SCALE
        shp = sp.shape
        sp = sp.reshape(shp[:3] + (-1,))
        sc = jnp.einsum('ngrd,nsgd->ngrs', qt, k_new).astype(jnp.float32) * SCALE
        sc = jnp.where(jpos <= t, sc, NEG_INF)
        p = jax.nn.softmax(jnp.concatenate([sp, sc], axis=-1), axis=-1)
        npast = sp.shape[-1]
        o_past = jnp.einsum('ngrkb,ngkbd->ngrd', p[..., :npast].reshape(shp).astype(vg.dtype), vg)
        o_cur = jnp.einsum('ngrs,nsgd->ngrd', p[..., npast:].astype(v_new.dtype), v_new)
        return o_past + o_cur

    out = lax.map(step, (jnp.moveaxis(q, 1, 0), jnp.moveaxis(phys, 1, 0), jpos))
    return jnp.moveaxis(out, 0, 1)


def window_prompt(q, k, v):
    n, t = q.shape[:2]
    nq = t // Q_BLOCK
    span = Q_BLOCK + WINDOW
    pad = ((0, 0), (WINDOW, 0), (0, 0), (0, 0))
    kp = jnp.pad(k, pad)
    vp = jnp.pad(v, pad)
    start = jnp.arange(nq) * Q_BLOCK
    kidx = start[:, None] + jnp.arange(span)[None, :]
    kb = kp[:, kidx]
    vb = vp[:, kidx]
    kpos = kidx - WINDOW
    qpos = start[:, None] + jnp.arange(Q_BLOCK)[None, :]
    d = qpos[:, :, None] - kpos[:, None, :]
    mask = (d >= 0) & (d < WINDOW) & (kpos[:, None, :] >= 0)
    qb = q.reshape(n, nq, Q_BLOCK, NSA_GROUPS, NSA_REP, HEAD_DIM)
    s = jnp.einsum('nqtgrd,nqsgd->nqgrts', qb, kb).astype(jnp.float32) * SCALE
    p = jax.nn.softmax(jnp.where(mask[None, :, None, None], s, NEG_INF), axis=-1)
    o = jnp.einsum('nqgrts,nqsgd->nqtgrd', p.astype(vb.dtype), vb)
    return o.reshape(q.shape)


def window_sample(q, k_new, v_new, buf_k, buf_v, past_len):
    n, s = q.shape[:2]
    wb = buf_k.shape[1]
    kk = jnp.concatenate([buf_k, k_new], axis=1)
    vv = jnp.concatenate([buf_v, v_new], axis=1)
    kpos = jnp.concatenate([past_len - wb + jnp.arange(wb), past_len + jnp.arange(s)])
    qpos = past_len + jnp.arange(s)
    d = qpos[:, None] - kpos[None, :]
    mask = (d >= 0) & (d < WINDOW)
    sc = jnp.einsum('nsgrd,nkgd->nsgrk', q, kk).astype(jnp.float32) * SCALE
    p = jax.nn.softmax(jnp.where(mask[None, :, None, None, :], sc, NEG_INF), axis=-1)
    o = jnp.einsum('nsgrk,nkgd->nsgrd', p.astype(vv.dtype), vv)
    keep = min(WINDOW, past_len + s)
    return o, kk[:, -keep:], vv[:, -keep:]


def diff_prompt(q, k, v, lam):
    n, t = q.shape[:2]
    nq = t // Q_BLOCK
    qc = jnp.moveaxis(q.reshape((n, nq, Q_BLOCK) + q.shape[2:]), 1, 0)
    kpos = jnp.arange(t)

    def blk(args):
        qb, start = args
        s = jnp.einsum('nqhcd,nkhcd->nchqk', qb, k).astype(jnp.float32) * SCALE
        mask = kpos[None, :] <= (start + jnp.arange(Q_BLOCK))[:, None]
        a = jax.nn.softmax(jnp.where(mask, s, NEG_INF), axis=-1)
        w = a[:, 0] - lam * a[:, 1]
        return jnp.einsum('nhqk,nkhe->nqhe', w.astype(v.dtype), v)

    out = lax.map(blk, (qc, jnp.arange(nq) * Q_BLOCK))
    return jnp.moveaxis(out, 0, 1).reshape(n, t, DIFF_HEADS, DIFF_VDIM)


def diff_partial(q, k, v, mask):
    s = jnp.einsum('nshcd,nkhcd->nchsk', q, k).astype(jnp.float32) * SCALE
    if mask is not None:
        s = jnp.where(mask, s, NEG_INF)
    m = s.max(axis=-1)
    e = jnp.exp(s - m[..., None])
    return m, e.sum(axis=-1), jnp.einsum('nchsk,nkhe->nchse', e, v.astype(jnp.float32))


def diff_sample(q, k_new, v_new, pool_k, pool_v, layer, page_table, lam):
    s = q.shape[1]

    def one_page(cols):
        kp = pool_k[layer, cols]
        kp = kp.reshape(kp.shape[:3] + (2, HEAD_DIM))
        return diff_partial(q, kp, pool_v[layer, cols], None)

    m_p, l_p, o_p = lax.map(one_page, page_table.T)
    causal = jnp.arange(s)[None, :] <= jnp.arange(s)[:, None]
    m_n, l_n, o_n = diff_partial(q, k_new, v_new, causal)
    m_all = jnp.concatenate([m_p, m_n[None]], axis=0)
    l_all = jnp.concatenate([l_p, l_n[None]], axis=0)
    o_all = jnp.concatenate([o_p, o_n[None]], axis=0)
    w = jnp.exp(m_all - m_all.max(axis=0))
    o = (w[..., None] * o_all).sum(axis=0) / (w * l_all).sum(axis=0)[..., None]
    out = o[:, 0] - lam * o[:, 1]
    return jnp.transpose(out, (0, 2, 1, 3)).astype(q.dtype)


def mix_out(x, pr, o_cmp, o_sel, o_win, o_diff, diff_norm_g, lam_init, w_proj_nsa, w_proj_diff, w_out):
    n, s, _ = x.shape
    g = pr['gates']
    o_nsa = g[..., 0:1] * o_cmp + g[..., 1:2] * o_sel + g[..., 2:3] * o_win
    o_nsa = o_nsa.reshape(n, s, NSA_Q_W) * jax.nn.silu(pr['z_nsa'])
    od = rms_norm(o_diff, diff_norm_g) * (1.0 - lam_init)
    od = od.reshape(n, s, DIFF_W) * jax.nn.silu(pr['z_diff'])
    merged = jax.nn.sigmoid(pr['g_a']) * (o_nsa @ w_proj_nsa) + jax.nn.sigmoid(pr['g_b']) * (od @ w_proj_diff)
    return x + merged @ w_out


def setup_inputs(seed: int = 0) -> dict:
    key = jax.random.key(seed)
    ks = iter(jax.random.split(key, 40))
    f32 = jnp.float32
    n_pages = PAST_LEN // PAGE_SIZE
    n_pool = (DEC_BATCH * n_pages * 5) // 4
    w_buf = min(WINDOW, PAST_LEN)

    def nrm(shape, scale=1.0):
        return jax.random.normal(next(ks), shape, f32) * scale

    kv_page = (DEPTH, n_pool, PAGE_SIZE, NSA_GROUPS, HEAD_DIM)
    diff_page = (DEPTH, n_pool, PAGE_SIZE, DIFF_HEADS, 2 * HEAD_DIM)
    win = (DEPTH, DEC_BATCH, w_buf, NSA_GROUPS, HEAD_DIM)
    perm = jax.random.permutation(next(ks), n_pool)
    page_table = perm[:DEC_BATCH * n_pages].reshape(DEC_BATCH, n_pages).astype(jnp.int32)
    cmp_in = CMP_BLOCK * HEAD_DIM
    return {
        'x_prompt': nrm((BATCH, SEQ, D_MODEL)),
        'x_sample': nrm((DEC_BATCH, DEC_SEQ, D_MODEL)),
        'cache_nsa_cmp_k': nrm(kv_page),
        'cache_nsa_cmp_v': nrm(kv_page),
        'cache_nsa_sel_k': nrm(kv_page),
        'cache_nsa_sel_v': nrm(kv_page),
        'state_nsa_win_k': nrm(win),
        'state_nsa_win_v': nrm(win),
        'cache_diff_k': nrm(diff_page),
        'cache_diff_v': nrm(diff_page),
        'page_table': page_table,
        'norm_g': 1.0 + nrm((DEPTH, D_MODEL), 0.01),
        'w_in': nrm((DEPTH, D_MODEL, IN_COLS), D_MODEL ** -0.5),
        'cmp_k_w1': nrm((DEPTH, cmp_in, CMP_HIDDEN), cmp_in ** -0.5),
        'cmp_k_pe': nrm((DEPTH, CMP_BLOCK, HEAD_DIM), 0.1),
        'cmp_k_w2': nrm((DEPTH, CMP_HIDDEN, HEAD_DIM), CMP_HIDDEN ** -0.5),
        'cmp_v_w1': nrm((DEPTH, cmp_in, CMP_HIDDEN), cmp_in ** -0.5),
        'cmp_v_pe': nrm((DEPTH, CMP_BLOCK, HEAD_DIM), 0.1),
        'cmp_v_w2': nrm((DEPTH, CMP_HIDDEN, HEAD_DIM), CMP_HIDDEN ** -0.5),
        'diff_lq1': nrm((DEPTH, HEAD_DIM), 0.1),
        'diff_lk1': nrm((DEPTH, HEAD_DIM), 0.1),
        'diff_lq2': nrm((DEPTH, HEAD_DIM), 0.1),
        'diff_lk2': nrm((DEPTH, HEAD_DIM), 0.1),
        'diff_norm_g': 1.0 + nrm((DEPTH, DIFF_VDIM), 0.01),
        'w_proj_nsa': nrm((DEPTH, NSA_Q_W, D_MODEL), NSA_Q_W ** -0.5),
        'w_proj_diff': nrm((DEPTH, DIFF_W, D_MODEL), DIFF_W ** -0.5),
        'w_out': nrm((DEPTH, D_MODEL, D_MODEL), D_MODEL ** -0.5),
        'final_norm_g': 1.0 + nrm((D_MODEL,), 0.01),
    }


def reference(x_prompt, x_sample, cache_nsa_cmp_k, cache_nsa_cmp_v, cache_nsa_sel_k, cache_nsa_sel_v,
              state_nsa_win_k, state_nsa_win_v, cache_diff_k, cache_diff_v, page_table,
              norm_g, w_in, cmp_k_w1, cmp_k_pe, cmp_k_w2, cmp_v_w1, cmp_v_pe, cmp_v_w2,
              diff_lq1, diff_lk1, diff_lq2, diff_lk2, diff_norm_g, w_proj_nsa, w_proj_diff, w_out,
              final_norm_g):
    seq = x_prompt.shape[1]
    n_s, dec_seq, _ = x_sample.shape
    past_len = page_table.shape[1] * PAGE_SIZE
    pos_p = jnp.arange(seq)
    pos_s = past_len + jnp.arange(dec_seq)
    hp, hs = x_prompt, x_sample
    prompt_new = [[] for _ in range(8)]
    sample_new = [[] for _ in range(8)]
    for l in range(DEPTH):
        lam_init = 0.8 - 0.6 * math.exp(-0.3 * l)
        lam = (jnp.exp(jnp.sum(diff_lq1[l].astype(jnp.float32) * diff_lk1[l].astype(jnp.float32)))
               - jnp.exp(jnp.sum(diff_lq2[l].astype(jnp.float32) * diff_lk2[l].astype(jnp.float32)))
               + lam_init)
        kw = (cmp_k_w1[l], cmp_k_pe[l], cmp_k_w2[l])
        vw = (cmp_v_w1[l], cmp_v_pe[l], cmp_v_w2[l])
        out_w = (diff_norm_g[l], lam_init, w_proj_nsa[l], w_proj_diff[l], w_out[l])

        pr = project(hp, norm_g[l], w_in[l], pos_p)
        kc = compress(pr['k_cmp'], *kw)
        vc = compress(pr['v_cmp'], *vw)
        o_cmp, p_cmp = cmp_attend(pr['q'], kc, vc, pos_p)
        idx, valid = select_prompt(block_scores(p_cmp, seq // SEL_BLOCK), pos_p)
        o_sel = sel_attend_prompt(pr['q'], pr['k_sel'], pr['v_sel'], idx, valid, pos_p)
        o_win = window_prompt(pr['q'], pr['k_win'], pr['v_win'])
        o_diff = diff_prompt(pr['dq'], pr['dk'], pr['dv'], lam)
        hp = mix_out(hp, pr, o_cmp, o_sel, o_win, o_diff, *out_w)
        keep_p = min(WINDOW, seq)
        p_rows = (pr['k_cmp'], pr['v_cmp'], pr['k_sel'], pr['v_sel'],
                  pr['k_win'][:, -keep_p:], pr['v_win'][:, -keep_p:],
                  pr['dk'].reshape(pr['dk'].shape[:3] + (2 * HEAD_DIM,)), pr['dv'])
        for lst, arr in zip(prompt_new, p_rows):
            lst.append(arr)

        sr = project(hs, norm_g[l], w_in[l], pos_s)
        k_rows = jnp.concatenate(
            [cache_nsa_cmp_k[l, page_table].reshape(n_s, past_len, NSA_GROUPS, HEAD_DIM), sr['k_cmp']], axis=1)
        v_rows = jnp.concatenate(
            [cache_nsa_cmp_v[l, page_table].reshape(n_s, past_len, NSA_GROUPS, HEAD_DIM), sr['v_cmp']], axis=1)
        kc_s = compress(k_rows, *kw)
        vc_s = compress(v_rows, *vw)
        o_cmp_s, p_cmp_s = cmp_attend(sr['q'], kc_s, vc_s, pos_s)
        idx_s = select_sample(block_scores(p_cmp_s, past_len // SEL_BLOCK))
        o_sel_s = sel_attend_sample(sr['q'], sr['k_sel'], sr['v_sel'], idx_s,
                                    cache_nsa_sel_k, cache_nsa_sel_v, l, page_table)
        o_win_s, win_k_s, win_v_s = window_sample(sr['q'], sr['k_win'], sr['v_win'],
                                                  state_nsa_win_k[l], state_nsa_win_v[l], past_len)
        o_diff_s = diff_sample(sr['dq'], sr['dk'], sr['dv'], cache_diff_k, cache_diff_v, l, page_table, lam)
        hs = mix_out(hs, sr, o_cmp_s, o_sel_s, o_win_s, o_diff_s, *out_w)
        s_rows = (sr['k_cmp'], sr['v_cmp'], sr['k_sel'], sr['v_sel'], win_k_s, win_v_s,
                  sr['dk'].reshape(sr['dk'].shape[:3] + (2 * HEAD_DIM,)), sr['dv'])
        for lst, arr in zip(sample_new, s_rows):
            lst.append(arr)

    y_prompt = rms_norm(hp, final_norm_g)
    y_sample = rms_norm(hs, final_norm_g)
    p_cmp_k, p_cmp_v, p_sel_k, p_sel_v, p_win_k, p_win_v, p_diff_k, p_diff_v = [jnp.stack(a, 0) for a in prompt_new]
    s_cmp_k, s_cmp_v, s_sel_k, s_sel_v, s_win_k, s_win_v, s_diff_k, s_diff_v = [jnp.stack(a, 0) for a in sample_new]
    return (y_prompt, y_sample, p_cmp_k, p_cmp_v, p_sel_k, p_sel_v, p_win_k, p_win_v, p_diff_k, p_diff_v,
            s_cmp_k, s_cmp_v, s_sel_k, s_sel_v, s_win_k, s_win_v, s_diff_k, s_diff_v)
```

```python
import functools
import math

import jax
import jax.numpy as jnp
from jax import lax
from jax.experimental import pallas as pl
from jax.experimental.pallas import tpu as pltpu

F32 = jnp.float32
BF16 = jnp.bfloat16

HEAD_DIM = 128
NSA_GROUPS = 4
NSA_REP = 4
NSA_HEADS = NSA_GROUPS * NSA_REP
CMP_BLOCK = 32
CMP_STRIDE = 16
CMP_HIDDEN = 2 * HEAD_DIM
SEL_BLOCK = 64
N_SELECT = 16
WINDOW = 512
DIFF_HEADS = 8
DIFF_VDIM = 2 * HEAD_DIM
ROPE_THETA = 10000.0
NORM_EPS = 1e-6
FORCE_SCORE = 1e4
NEG_INF = -1e30
SCALE = HEAD_DIM ** -0.5
PAGE_SIZE = 128

LANES = 128
NSA_Q_W = NSA_HEADS * HEAD_DIM
NSA_KV_W = NSA_GROUPS * HEAD_DIM
DIFF_W = DIFF_HEADS * DIFF_VDIM
PAD_ROWS = 128

COL_Q = 0
COL_Z_NSA = COL_Q + NSA_Q_W
COL_DQ = COL_Z_NSA + NSA_Q_W
COL_DK = COL_DQ + DIFF_W
COL_DV = COL_DK + DIFF_W
COL_Z_DIFF = COL_DV + DIFF_W
COL_KV = COL_Z_DIFF + DIFF_W
COL_GA = COL_KV + 6 * NSA_KV_W
PROJ_TN = 512
VMEM_LIMIT = 56 * 1024 * 1024


def _sigmoid(x):
    return 1.0 / (1.0 + jnp.exp(-x))


def _dot(a, b):
    return jnp.dot(a, b, preferred_element_type=F32)


def _dot_nt(a, b):
    return lax.dot_general(a, b, (((1,), (1,)), ((), ())), preferred_element_type=F32)


def _params(*sem):
    return pltpu.CompilerParams(dimension_semantics=sem, vmem_limit_bytes=VMEM_LIMIT)


def _rmsnorm_kernel(x_ref, g_ref, o_ref):
    x = x_ref[...]
    ms = jnp.mean(x * x, axis=-1, keepdims=True)
    o_ref[...] = (x * lax.rsqrt(ms + NORM_EPS) * g_ref[...]).astype(o_ref.dtype)


def _rmsnorm(x2d, g, out_dtype):
    m, d = x2d.shape
    tm = min(256, m)
    return pl.pallas_call(
        _rmsnorm_kernel,
        grid=(m // tm,),
        in_specs=[pl.BlockSpec((tm, d), lambda i: (i, 0)), pl.BlockSpec((1, d), lambda i: (0, 0))],
        out_specs=pl.BlockSpec((tm, d), lambda i: (i, 0)),
        out_shape=jax.ShapeDtypeStruct((m, d), out_dtype),
        compiler_params=_params("parallel"),
        name="rmsnorm",
    )(x2d, g.reshape(1, d))


def _proj_layout(d_model):
    col_gb = COL_GA + d_model
    col_gates = col_gb + d_model
    n_cols = col_gates + PROJ_TN
    return col_gb, col_gates, n_cols


def _pack_w_in(w, d_model):
    splits = (NSA_Q_W, 6 * NSA_KV_W, 3 * NSA_HEADS, NSA_Q_W, DIFF_W, DIFF_W, DIFF_W, DIFF_W, d_model, d_model)
    cuts = [sum(splits[:i + 1]) for i in range(len(splits) - 1)]
    q, kv, gates, z_nsa, dq, dk, dv, z_diff, g_a, g_b = jnp.split(w, cuts, axis=1)
    gates = gates.reshape(w.shape[0], NSA_GROUPS, NSA_REP * 3)
    gates = jnp.pad(gates, ((0, 0), (0, 0), (0, LANES - NSA_REP * 3))).reshape(w.shape[0], NSA_GROUPS * LANES)
    return jnp.concatenate([q, z_nsa, dq, dk, dv, z_diff, kv, g_a, g_b, gates], axis=1).astype(BF16)


def _proj_kernel(h_ref, w_ref, cos_ref, sin_ref, o_ref, *, rope_blocks, silu_blocks, sig_start):
    j = pl.program_id(1)
    o_ref[...] = _dot(h_ref[...], w_ref[...])

    def _in(blocks):
        pred = None
        for lo, hi in blocks:
            c = (j >= lo) & (j < hi)
            pred = c if pred is None else (pred | c)
        return pred

    @pl.when(_in(rope_blocks))
    def _():
        cos = cos_ref[...]
        sin = sin_ref[...]
        for c in range(o_ref.shape[1] // HEAD_DIM):
            sl = slice(c * HEAD_DIM, (c + 1) * HEAD_DIM)
            x = o_ref[:, sl]
            o_ref[:, sl] = x * cos + pltpu.roll(x, HEAD_DIM // 2, 1) * sin

    @pl.when(_in(silu_blocks))
    def _():
        x = o_ref[...]
        o_ref[...] = x * _sigmoid(x)

    @pl.when(j >= sig_start)
    def _():
        o_ref[...] = _sigmoid(o_ref[...])


def _project(h, w_packed, cos, sin):
    m, d = h.shape
    n_cols = w_packed.shape[1]
    tm = min(512, m)
    tn = PROJ_TN
    b = lambda col: col // tn
    kv0 = b(COL_KV)
    rope_blocks = ((b(COL_Q), b(COL_Z_NSA)), (b(COL_DQ), b(COL_DV)),
                   (kv0, kv0 + 1), (kv0 + 2, kv0 + 3), (kv0 + 4, kv0 + 5))
    silu_blocks = ((b(COL_Z_NSA), b(COL_DQ)), (b(COL_Z_DIFF), b(COL_KV)))
    kern = functools.partial(_proj_kernel, rope_blocks=rope_blocks, silu_blocks=silu_blocks, sig_start=b(COL_GA))
    return pl.pallas_call(
        kern,
        grid=(m // tm, n_cols // tn),
        in_specs=[pl.BlockSpec((tm, d), lambda i, j: (i, 0)),
                  pl.BlockSpec((d, tn), lambda i, j: (0, j)),
                  pl.BlockSpec((tm, HEAD_DIM), lambda i, j: (i, 0)),
                  pl.BlockSpec((tm, HEAD_DIM), lambda i, j: (i, 0))],
        out_specs=pl.BlockSpec((tm, tn), lambda i, j: (i, j)),
        out_shape=jax.ShapeDtypeStruct((m, n_cols), F32),
        compiler_params=_params("parallel", "arbitrary"),
        name="in_proj",
    )(h, w_packed, cos, sin)


def _rope_tables(pos):
    half = HEAD_DIM // 2
    inv = ROPE_THETA ** (-jnp.arange(half, dtype=F32) / half)
    ang = pos.astype(F32)[:, None] * inv[None, :]
    cos, sin = jnp.cos(ang), jnp.sin(ang)
    return jnp.concatenate([cos, cos], axis=1), jnp.concatenate([-sin, sin], axis=1)


def _pos_term_kernel(pe_ref, w1_ref, o_ref):
    pe = jnp.broadcast_to(pe_ref[0], (8, pe_ref.shape[2])).astype(BF16)
    o_ref[0] = _dot(pe, w1_ref[0])


def _pos_terms(pe_kv, w1_kv):
    kdim = pe_kv.shape[1]
    out = pl.pallas_call(
        _pos_term_kernel,
        grid=(2,),
        in_specs=[pl.BlockSpec((1, 1, kdim), lambda i: (i, 0, 0)),
                  pl.BlockSpec((1, kdim, CMP_HIDDEN), lambda i: (i, 0, 0))],
        out_specs=pl.BlockSpec((1, 8, CMP_HIDDEN), lambda i: (i, 0, 0)),
        out_shape=jax.ShapeDtypeStruct((2, 8, CMP_HIDDEN), F32),
        compiler_params=_params("arbitrary"),
        name="cmp_pos_term",
    )(pe_kv.reshape(2, 1, kdim), w1_kv)
    return out[:, 0:1, :]


def _finish_compress(first, second, pos, w2):
    rows = first.shape[0]
    shifted = pltpu.roll(second, rows - 1, 0)
    pre = first + shifted + pos
    hid = pre * _sigmoid(pre)
    return _dot(hid.astype(BF16), w2)


def _compress_prompt_kernel(x0_ref, x1_ref, x2_ref, x3_ref, w1_ref, pos_ref, w2_ref, o_ref, *, nh):
    x_refs = (x0_ref, x1_ref, x2_ref, x3_ref)
    acc = jnp.zeros((NSA_GROUPS * nh, 2 * CMP_HIDDEN), F32)
    for s in range(CMP_STRIDE):
        xs = jnp.concatenate([x[pl.ds(s, nh, stride=CMP_STRIDE), :] for x in x_refs], axis=0)
        acc = acc + _dot(xs.astype(BF16), w1_ref[0, s])
    out = _finish_compress(acc[:, :CMP_HIDDEN], acc[:, CMP_HIDDEN:], pos_ref[0], w2_ref[0])
    o_ref[0, 0] = out.reshape(NSA_GROUPS, nh, HEAD_DIM)


def _compress_prompt(proj, w1ab_kv, pos_kv, w2_kv, n, seq):
    nh = seq // CMP_STRIDE
    cb = COL_KV // HEAD_DIM
    kern = functools.partial(_compress_prompt_kernel, nh=nh)

    def group_spec(g):
        return pl.BlockSpec((seq, HEAD_DIM), lambda c, i: (i, cb + c * NSA_GROUPS + g))

    return pl.pallas_call(
        kern,
        grid=(2, n),
        in_specs=[group_spec(g) for g in range(NSA_GROUPS)]
        + [pl.BlockSpec((1, CMP_STRIDE, HEAD_DIM, 2 * CMP_HIDDEN), lambda c, i: (c, 0, 0, 0)),
           pl.BlockSpec((1, 1, CMP_HIDDEN), lambda c, i: (c, 0, 0)),
           pl.BlockSpec((1, CMP_HIDDEN, HEAD_DIM), lambda c, i: (c, 0, 0))],
        out_specs=pl.BlockSpec((1, 1, NSA_GROUPS, nh, HEAD_DIM), lambda c, i: (c, i, 0, 0, 0)),
        out_shape=jax.ShapeDtypeStruct((2, n, NSA_GROUPS, nh, HEAD_DIM), F32),
        compiler_params=_params("arbitrary", "arbitrary"),
        name="compress_prompt",
    )(proj, proj, proj, proj, w1ab_kv, pos_kv, w2_kv)


def _cmp_probs(qs, kc, trow, nc):
    s = _dot_nt(qs, kc) * SCALE
    cidx = lax.broadcasted_iota(jnp.int32, s.shape, 1)
    vmask = ((cidx * CMP_STRIDE + CMP_BLOCK) <= trow + 1) & (cidx < nc)
    sm = jnp.where(vmask, s, NEG_INF)
    e = jnp.exp(sm - jnp.max(sm, axis=-1, keepdims=True))
    p = e / jnp.sum(e, axis=-1, keepdims=True)
    return jnp.where(vmask, p, 0.0)


def _block_scores(psum, nblk):
    nc_rows = psum.shape[1]
    cc = lax.broadcasted_iota(jnp.int32, (nc_rows, LANES), 0) * CMP_STRIDE
    jj = lax.broadcasted_iota(jnp.int32, (nc_rows, LANES), 1)
    cover = (cc < jj * SEL_BLOCK + SEL_BLOCK) & (cc + CMP_BLOCK > jj * SEL_BLOCK) & (jj < nblk)
    cover = jnp.where(cover, 1.0, 0.0).astype(BF16)
    hi = psum.astype(BF16)
    r1 = psum - hi.astype(F32)
    mid = r1.astype(BF16)
    lo = (r1 - mid.astype(F32)).astype(BF16)
    return _dot(hi, cover) + _dot(mid, cover) + _dot(lo, cover)


def _rank_select(ranked, nblk, nsel):
    jl = lax.broadcasted_iota(jnp.int32, ranked.shape, 1)
    ranked = jnp.where(jl < nblk, ranked, -2.0)
    rank = jnp.zeros(ranked.shape, F32)
    for i in range(nblk):
        col = ranked[:, i:i + 1]
        beats = (col > ranked) | ((col == ranked) & (jl > i))
        rank = rank + jnp.where(beats, 1.0, 0.0)
    return (rank < nsel) & (jl < nblk)


def _expand_blocks(sel01, n_keys):
    jj = lax.broadcasted_iota(jnp.int32, (LANES, n_keys), 0)
    kk = lax.broadcasted_iota(jnp.int32, (LANES, n_keys), 1)
    expand = jnp.where((kk // SEL_BLOCK) == jj, 1.0, 0.0).astype(BF16)
    return _dot(sel01, expand)


def _masked_attend(qs, k, v, mask):
    s = jnp.where(mask, _dot_nt(qs, k) * SCALE, NEG_INF)
    e = jnp.exp(s - jnp.max(s, axis=-1, keepdims=True))
    return _dot(e.astype(BF16), v) / jnp.sum(e, axis=-1, keepdims=True)


def _nsa_prompt_kernel(q_ref, kc_ref, vc_ref, ks_ref, vs_ref, kw_ref, vw_ref, gate_ref, z_ref, o_ref,
                       *, tq, seq, nc, nblk, nsel):
    qb = pl.program_id(2)
    q0 = qb * tq
    q = q_ref[...]
    qs = jnp.concatenate([q[:, r * HEAD_DIM:(r + 1) * HEAD_DIM] for r in range(NSA_REP)], axis=0).astype(BF16)

    kc = kc_ref[0, 0, 0].astype(BF16)
    vc = vc_ref[0, 0, 0].astype(BF16)
    ncp = kc.shape[0]
    trow = (lax.broadcasted_iota(jnp.int32, (NSA_REP * tq, ncp), 0) & (tq - 1)) + q0
    p = _cmp_probs(qs, kc, trow, nc)
    o_cmp = _dot(p.astype(BF16), vc)

    psum = p[0:tq] + p[tq:2 * tq] + p[2 * tq:3 * tq] + p[3 * tq:4 * tq]
    score = _block_scores(psum, nblk)
    jl = lax.broadcasted_iota(jnp.int32, (tq, LANES), 1)
    tl = lax.broadcasted_iota(jnp.int32, (tq, LANES), 0) + q0
    cur = tl // SEL_BLOCK
    forced = (jl == 0) | (jl == cur) | (jl == cur - 1)
    ranked = jnp.where(forced, FORCE_SCORE, jnp.where(jl <= cur, score, -1.0))
    sel = _rank_select(ranked, nblk, nsel) & (jl <= cur)
    key_sel = _expand_blocks(jnp.where(sel, 1.0, 0.0).astype(BF16), seq)
    kpos = lax.broadcasted_iota(jnp.int32, (tq, seq), 1)
    tpos = lax.broadcasted_iota(jnp.int32, (tq, seq), 0) + q0
    sel_mask = (key_sel > 0.5) & (kpos <= tpos)

    span = min(tq + WINDOW, seq)
    start = pl.multiple_of(jnp.clip(q0 + tq - span, 0, seq - span), tq)
    wk = lax.broadcasted_iota(jnp.int32, (tq, span), 1) + start
    wd = (lax.broadcasted_iota(jnp.int32, (tq, span), 0) + q0) - wk
    win_mask = (wd >= 0) & (wd < WINDOW)

    ks = ks_ref[...].astype(BF16)
    vs = vs_ref[...].astype(BF16)
    kw = kw_ref[pl.ds(start, span), :].astype(BF16)
    vw = vw_ref[pl.ds(start, span), :].astype(BF16)
    gt = gate_ref[...]
    z = z_ref[...]
    for r in range(NSA_REP):
        qr = qs[r * tq:(r + 1) * tq]
        o_sel = _masked_attend(qr, ks, vs, sel_mask)
        o_win = _masked_attend(qr, kw, vw, win_mask)
        o = (gt[:, 3 * r:3 * r + 1] * o_cmp[r * tq:(r + 1) * tq] + gt[:, 3 * r + 1:3 * r + 2] * o_sel
             + gt[:, 3 * r + 2:3 * r + 3] * o_win)
        sl = slice(r * HEAD_DIM, (r + 1) * HEAD_DIM)
        o_ref[:, sl] = (o * z[:, sl]).astype(o_ref.dtype)


def _nsa_prompt(proj, kvc, col_gates, n, seq):
    tq = 128
    nqb = seq // tq
    nh = kvc.shape[3]
    nblk = seq // SEL_BLOCK
    kern = functools.partial(_nsa_prompt_kernel, tq=tq, seq=seq, nc=nh - 1, nblk=nblk, nsel=min(N_SELECT, nblk))
    gw = NSA_REP * HEAD_DIM
    kvb = COL_KV // HEAD_DIM

    def kv_spec(which):
        return pl.BlockSpec((seq, HEAD_DIM), lambda i, g, qb: (i, kvb + which * NSA_GROUPS + g))

    return pl.pallas_call(
        kern,
        grid=(n, NSA_GROUPS, nqb),
        in_specs=[pl.BlockSpec((tq, gw), lambda i, g, qb: (i * nqb + qb, COL_Q // gw + g)),
                  pl.BlockSpec((1, 1, 1, nh, HEAD_DIM), lambda i, g, qb: (0, i, g, 0, 0)),
                  pl.BlockSpec((1, 1, 1, nh, HEAD_DIM), lambda i, g, qb: (1, i, g, 0, 0)),
                  kv_spec(2), kv_spec(3), kv_spec(4), kv_spec(5),
                  pl.BlockSpec((tq, LANES), lambda i, g, qb: (i * nqb + qb, col_gates // LANES + g)),
                  pl.BlockSpec((tq, gw), lambda i, g, qb: (i * nqb + qb, COL_Z_NSA // gw + g))],
        out_specs=pl.BlockSpec((tq, gw), lambda i, g, qb: (i * nqb + qb, g)),
        out_shape=jax.ShapeDtypeStruct((n * seq, NSA_Q_W), BF16),
        compiler_params=_params("parallel", "parallel", "arbitrary"),
        name="nsa_prompt",
    )(proj, kvc, kvc, proj, proj, proj, proj, proj, proj)


def _lambda(lq1_ref, lk1_ref, lq2_ref, lk2_ref, lam_init):
    a = jnp.sum(lq1_ref[...] * lk1_ref[...], axis=-1, keepdims=True)
    b = jnp.sum(lq2_ref[...] * lk2_ref[...], axis=-1, keepdims=True)
    return jnp.exp(a) - jnp.exp(b) + lam_init


def _diff_finish(o, g, z, lam_init):
    ms = jnp.mean(o * o, axis=-1, keepdims=True)
    return o * lax.rsqrt(ms + NORM_EPS) * g * (1.0 - lam_init) * z


def _diff_prompt_kernel(lq1_ref, lk1_ref, lq2_ref, lk2_ref, q_ref, k_ref, v_ref, z_ref, g_ref, o_ref,
                        *, tq, seq, lam_init):
    q0 = pl.program_id(2) * tq
    lam = _lambda(lq1_ref, lk1_ref, lq2_ref, lk2_ref, lam_init)
    causal = lax.broadcasted_iota(jnp.int32, (tq, seq), 1) <= lax.broadcasted_iota(jnp.int32, (tq, seq), 0) + q0
    q = q_ref[...].astype(BF16)
    k = k_ref[...].astype(BF16)

    def attn_map(c):
        sl = slice(c * HEAD_DIM, (c + 1) * HEAD_DIM)
        s = jnp.where(causal, _dot_nt(q[:, sl], k[:, sl]) * SCALE, NEG_INF)
        e = jnp.exp(s - jnp.max(s, axis=-1, keepdims=True))
        return e / jnp.sum(e, axis=-1, keepdims=True)

    w = attn_map(0) - lam * attn_map(1)
    o = _dot(w.astype(BF16), v_ref[...].astype(BF16))
    o_ref[...] = _diff_finish(o, g_ref[...], z_ref[...], lam_init).astype(o_ref.dtype)


def _diff_prompt(proj, lams, diff_g, lam_init, n, seq):
    tq = 256
    nqb = seq // tq
    kern = functools.partial(_diff_prompt_kernel, tq=tq, seq=seq, lam_init=lam_init)
    hw = DIFF_VDIM
    lam_spec = pl.BlockSpec((1, HEAD_DIM), lambda i, h, qb: (0, 0))
    return pl.pallas_call(
        kern,
        grid=(n, DIFF_HEADS, nqb),
        in_specs=[lam_spec, lam_spec, lam_spec, lam_spec,
                  pl.BlockSpec((tq, hw), lambda i, h, qb: (i * nqb + qb, COL_DQ // hw + h)),
                  pl.BlockSpec((seq, hw), lambda i, h, qb: (i, COL_DK // hw + h)),
                  pl.BlockSpec((seq, hw), lambda i, h, qb: (i, COL_DV // hw + h)),
                  pl.BlockSpec((tq, hw), lambda i, h, qb: (i * nqb + qb, COL_Z_DIFF // hw + h)),
                  pl.BlockSpec((1, hw), lambda i, h, qb: (0, 0))],
        out_specs=pl.BlockSpec((tq, hw), lambda i, h, qb: (i * nqb + qb, h)),
        out_shape=jax.ShapeDtypeStruct((n * seq, DIFF_W), BF16),
        compiler_params=_params("parallel", "parallel", "arbitrary"),
        name="diff_prompt",
    )(*lams, proj, proj, proj, proj, diff_g)


def _mix1_kernel(a_ref, wa_ref, b_ref, wb_ref, ga_ref, gb_ref, o_ref):
    pa = _dot(a_ref[...].astype(BF16), wa_ref[...])
    pb = _dot(b_ref[...].astype(BF16), wb_ref[...])
    o_ref[...] = (ga_ref[...] * pa + gb_ref[...] * pb).astype(o_ref.dtype)


def _mix1(o_nsa, w_nsa, o_diff, w_diff, proj, col_gb):
    m = o_nsa.shape[0]
    d_model = w_nsa.shape[1]
    tm = min(512, m)
    tn = 512
    return pl.pallas_call(
        _mix1_kernel,
        grid=(m // tm, d_model // tn),
        in_specs=[pl.BlockSpec((tm, NSA_Q_W), lambda i, j: (i, 0)),
                  pl.BlockSpec((NSA_Q_W, tn), lambda i, j: (0, j)),
                  pl.BlockSpec((tm, DIFF_W), lambda i, j: (i, 0)),
                  pl.BlockSpec((DIFF_W, tn), lambda i, j: (0, j)),
                  pl.BlockSpec((tm, tn), lambda i, j: (i, COL_GA // tn + j)),
                  pl.BlockSpec((tm, tn), lambda i, j: (i, col_gb // tn + j))],
        out_specs=pl.BlockSpec((tm, tn), lambda i, j: (i, j)),
        out_shape=jax.ShapeDtypeStruct((m, d_model), BF16),
        compiler_params=_params("parallel", "arbitrary"),
        name="mix_gated_proj",
    )(o_nsa, w_nsa, o_diff, w_diff, proj, proj)


def _mix2_kernel(m_ref, w_ref, x_ref, g_ref, o_ref):
    k = pl.program_id(1)
    part = _dot(m_ref[...], w_ref[...])

    @pl.when(k == 0)
    def _():
        o_ref[...] = part

    @pl.when(k > 0)
    def _():
        o_ref[...] += part

    @pl.when(k == pl.num_programs(1) - 1)
    def _():
        y = x_ref[...] + o_ref[...]
        ms = jnp.mean(y * y, axis=-1, keepdims=True)
        o_ref[...] = y * lax.rsqrt(ms + NORM_EPS) * g_ref[...]


def _mix2(merged, w_out, x2d, final_g):
    m, d_model = x2d.shape
    tm = min(256, m)
    tk = 512
    return pl.pallas_call(
        _mix2_kernel,
        grid=(m // tm, d_model // tk),
        in_specs=[pl.BlockSpec((tm, tk), lambda i, k: (i, k)),
                  pl.BlockSpec((tk, d_model), lambda i, k: (k, 0)),
                  pl.BlockSpec((tm, d_model), lambda i, k: (i, 0)),
                  pl.BlockSpec((1, d_model), lambda i, k: (0, 0))],
        out_specs=pl.BlockSpec((tm, d_model), lambda i, k: (i, 0)),
        out_shape=jax.ShapeDtypeStruct((m, d_model), F32),
        compiler_params=_params("parallel", "arbitrary"),
        name="out_proj_residual_norm",
    )(merged, w_out, x2d, final_g.reshape(1, d_model))


def _cmp_partial_kernel(pt_ref, *refs, nbs):
    del pt_ref
    x_refs, w_ref, o_ref = refs[:nbs], refs[nbs], refs[nbs + 1]
    hpp = PAGE_SIZE // CMP_STRIDE
    acc = jnp.zeros((nbs * NSA_GROUPS * hpp, 2 * CMP_HIDDEN), F32)
    for s in range(CMP_STRIDE):
        pieces = [x_refs[i][0, pl.ds(NSA_GROUPS * s + g, hpp, stride=NSA_GROUPS * CMP_STRIDE), :]
                  for i in range(nbs) for g in range(NSA_GROUPS)]
        xs = jnp.concatenate(pieces, axis=0).astype(BF16)
        acc = acc + _dot(xs, w_ref[s])
    o_ref[...] = acc.reshape(nbs, NSA_GROUPS, hpp, 2 * CMP_HIDDEN)


def _cmp_partials(cache, page_flat, w1ab, n, n_pages):
    pool = cache.shape[0]
    hpp = PAGE_SIZE // CMP_STRIDE
    view = cache.reshape(pool, PAGE_SIZE * NSA_GROUPS, HEAD_DIM)
    nbs = min(16, n)
    kern = functools.partial(_cmp_partial_kernel, nbs=nbs)

    def page_spec(i):
        return pl.BlockSpec((1, PAGE_SIZE * NSA_GROUPS, HEAD_DIM),
                            lambda nb, p, pt: (pt[(nb * nbs + i) * n_pages + p], 0, 0))

    return pl.pallas_call(
        kern,
        grid_spec=pltpu.PrefetchScalarGridSpec(
            num_scalar_prefetch=1,
            grid=(n // nbs, n_pages),
            in_specs=[page_spec(i) for i in range(nbs)]
            + [pl.BlockSpec((CMP_STRIDE, HEAD_DIM, 2 * CMP_HIDDEN), lambda nb, p, pt: (0, 0, 0))],
            out_specs=pl.BlockSpec((nbs, NSA_GROUPS, hpp, 2 * CMP_HIDDEN), lambda nb, p, pt: (nb, 0, p, 0)),
        ),
        out_shape=jax.ShapeDtypeStruct((n, NSA_GROUPS, n_pages * hpp, 2 * CMP_HIDDEN), F32),
        compiler_params=_params("parallel", "arbitrary"),
        name="cmp_partials_sample",
    )(page_flat, *([view] * nbs), w1ab)


def _cmp_sample_kernel(fsk_ref, fsv_ref, pos_ref, w2_ref, q_ref, ocmp_ref, sel_ref,
                       *, nb2, nh, nc, past_len, dec, nblk, nsel):
    rows = nb2 * NSA_GROUPS * nh

    def compress(fs_ref, c):
        fs = fs_ref[...].reshape(rows, 2 * CMP_HIDDEN)
        return _finish_compress(fs[:, :CMP_HIDDEN], fs[:, CMP_HIDDEN:], pos_ref[c], w2_ref[c])

    kc = compress(fsk_ref, 0).astype(BF16)
    vc = compress(fsv_ref, 1).astype(BF16)
    trow = (lax.broadcasted_iota(jnp.int32, (NSA_REP * dec, nh), 0) & (dec - 1)) + past_len
    scores = []
    for i in range(nb2):
        qi = q_ref[i * dec:(i + 1) * dec, :]
        for g in range(NSA_GROUPS):
            base = (i * NSA_GROUPS + g) * nh
            c0 = g * NSA_REP * HEAD_DIM
            qg = jnp.concatenate([qi[:, c0 + r * HEAD_DIM:c0 + (r + 1) * HEAD_DIM] for r in range(NSA_REP)],
                                 axis=0).astype(BF16)
            p = _cmp_probs(qg, kc[base:base + nh], trow, nc)
            o = _dot(p.astype(BF16), vc[base:base + nh])
            for r in range(NSA_REP):
                ocmp_ref[i * dec:(i + 1) * dec, c0 + r * HEAD_DIM:c0 + (r + 1) * HEAD_DIM] = o[r * dec:(r + 1) * dec]
            psum = p[0:dec] + p[dec:2 * dec] + p[2 * dec:3 * dec] + p[3 * dec:4 * dec]
            scores.append(_block_scores(psum, nblk))
    score = jnp.concatenate(scores, axis=0)
    jl = lax.broadcasted_iota(jnp.int32, score.shape, 1)
    ranked = jnp.where((jl == 0) | (jl == nblk - 1), FORCE_SCORE, score)
    sel = _rank_select(ranked, nblk, nsel)
    sel_ref[...] = jnp.where(sel, 1.0, 0.0).reshape(nb2, NSA_GROUPS, dec, LANES)


def _cmp_sample(fs_k, fs_v, pos_kv, w2_kv, proj_s, n, dec, past_len):
    nh = fs_k.shape[2]
    nb2 = min(4, n)
    nblk = past_len // SEL_BLOCK
    kern = functools.partial(_cmp_sample_kernel, nb2=nb2, nh=nh, nc=nh - 1, past_len=past_len, dec=dec,
                             nblk=nblk, nsel=min(N_SELECT - 1, nblk))
    fs_spec = pl.BlockSpec((nb2, NSA_GROUPS, nh, 2 * CMP_HIDDEN), lambda i: (i, 0, 0, 0))
    return pl.pallas_call(
        kern,
        grid=(n // nb2,),
        in_specs=[fs_spec, fs_spec,
                  pl.BlockSpec((2, 1, CMP_HIDDEN), lambda i: (0, 0, 0)),
                  pl.BlockSpec((2, CMP_HIDDEN, HEAD_DIM), lambda i: (0, 0, 0)),
                  pl.BlockSpec((nb2 * dec, NSA_Q_W), lambda i: (i, 0))],
        out_specs=[pl.BlockSpec((nb2 * dec, NSA_Q_W), lambda i: (i, 0)),
                   pl.BlockSpec((nb2, NSA_GROUPS, dec, LANES), lambda i: (i, 0, 0, 0))],
        out_shape=[jax.ShapeDtypeStruct((n * dec, NSA_Q_W), F32),
                   jax.ShapeDtypeStruct((n, NSA_GROUPS, dec, LANES), F32)],
        compiler_params=_params("parallel"),
        name="cmp_attend_select_sample",
    )(fs_k, fs_v, pos_kv, w2_kv, proj_s)


def _pad_new(x):
    return jnp.concatenate([x, jnp.zeros((PAD_ROWS - x.shape[0], x.shape[1]), x.dtype)], axis=0).astype(BF16)


def _nsa_sample_kernel(pt_ref, *refs, n_pages, past_len, dec, wb):
    del pt_ref
    ksel_pages = refs[:n_pages]
    vsel_pages = refs[n_pages:2 * n_pages]
    (bufk_ref, bufv_ref, q_ref, ksn_ref, vsn_ref, kwn_ref, vwn_ref, ocmp_ref, gate_ref, z_ref, sel_ref,
     o_ref, nbk_ref, nbv_ref) = refs[2 * n_pages:]
    rows = NSA_REP * dec
    t_of_row = lambda shape: lax.broadcasted_iota(jnp.int32, shape, 0) & (dec - 1)

    def group_rows(ref, g, n_tok):
        return ref[0, pl.ds(g, n_tok, stride=NSA_GROUPS), :]

    keep_rows = (wb - dec) * NSA_GROUPS
    for buf_ref, new_ref, out_ref in ((bufk_ref, kwn_ref, nbk_ref), (bufv_ref, vwn_ref, nbv_ref)):
        out_ref[0, 0:keep_rows, :] = buf_ref[0, dec * NSA_GROUPS:wb * NSA_GROUPS, :]
        for g in range(NSA_GROUPS):
            out_ref[0, pl.ds(keep_rows + g, dec, stride=NSA_GROUPS), :] = new_ref[:, g * HEAD_DIM:(g + 1) * HEAD_DIM]

    jn = lax.broadcasted_iota(jnp.int32, (rows, PAD_ROWS), 1)
    new_ok = jnp.where((jn < dec) & (jn <= t_of_row((rows, PAD_ROWS))), 1.0, 0.0)
    n_win = wb + PAD_ROWS
    wi = lax.broadcasted_iota(jnp.int32, (rows, n_win), 1)
    wkpos = jnp.where(wi < wb, past_len - wb + wi, past_len + wi - wb)
    wd = (past_len + t_of_row((rows, n_win))) - wkpos
    win_mask = (wd >= 0) & (wd < WINDOW) & (wi < wb + dec)

    q = q_ref[...]
    gt = gate_ref[...]
    z = z_ref[...]
    for g in range(NSA_GROUPS):
        sl = slice(g * HEAD_DIM, (g + 1) * HEAD_DIM)
        c0 = g * NSA_REP * HEAD_DIM
        qg = jnp.concatenate([q[:, c0 + r * HEAD_DIM:c0 + (r + 1) * HEAD_DIM] for r in range(NSA_REP)],
                             axis=0).astype(BF16)
        kg = jnp.concatenate([group_rows(pg, g, PAGE_SIZE).astype(BF16) for pg in ksel_pages]
                             + [_pad_new(ksn_ref[:, sl])], axis=0)
        vg = jnp.concatenate([group_rows(pg, g, PAGE_SIZE).astype(BF16) for pg in vsel_pages]
                             + [_pad_new(vsn_ref[:, sl])], axis=0)
        key_sel = _expand_blocks(sel_ref[0, g].astype(BF16), past_len)
        ok = jnp.concatenate([jnp.concatenate([key_sel] * NSA_REP, axis=0), new_ok], axis=1)
        o_sel = _masked_attend(qg, kg, vg, ok > 0.5)
        kw = jnp.concatenate([group_rows(bufk_ref, g, wb).astype(BF16), _pad_new(kwn_ref[:, sl])], axis=0)
        vw = jnp.concatenate([group_rows(bufv_ref, g, wb).astype(BF16), _pad_new(vwn_ref[:, sl])], axis=0)
        o_win = _masked_attend(qg, kw, vw, win_mask)
        for r in range(NSA_REP):
            cs = slice(c0 + r * HEAD_DIM, c0 + (r + 1) * HEAD_DIM)
            rs = slice(r * dec, (r + 1) * dec)
            gi = g * LANES + 3 * r
            o = (gt[:, gi:gi + 1] * ocmp_ref[:, cs] + gt[:, gi + 1:gi + 2] * o_sel[rs]
                 + gt[:, gi + 2:gi + 3] * o_win[rs])
            o_ref[:, cs] = o * z[:, cs]


def _nsa_sample(cache_k, cache_v, buf_k, buf_v, page_flat, proj_s, o_cmp, sel, col_gates, n, dec, n_pages):
    pool = cache_k.shape[0]
    wb = buf_k.shape[1]
    past_len = n_pages * PAGE_SIZE
    kview = cache_k.reshape(pool, PAGE_SIZE * NSA_GROUPS, HEAD_DIM)
    vview = cache_v.reshape(pool, PAGE_SIZE * NSA_GROUPS, HEAD_DIM)
    bk = buf_k.reshape(n, wb * NSA_GROUPS, HEAD_DIM)
    bv = buf_v.reshape(n, wb * NSA_GROUPS, HEAD_DIM)
    kern = functools.partial(_nsa_sample_kernel, n_pages=n_pages, past_len=past_len, dec=dec, wb=wb)
    kvb = COL_KV // NSA_KV_W

    def page_spec(p):
        return pl.BlockSpec((1, PAGE_SIZE * NSA_GROUPS, HEAD_DIM), lambda i, pt: (pt[i * n_pages + p], 0, 0))

    def row_spec(width, col_block):
        return pl.BlockSpec((dec, width), lambda i, pt: (i, col_block))

    buf_spec = pl.BlockSpec((1, wb * NSA_GROUPS, HEAD_DIM), lambda i, pt: (i, 0, 0))
    in_specs = ([page_spec(p) for p in range(n_pages)] + [page_spec(p) for p in range(n_pages)]
                + [buf_spec, buf_spec,
                   row_spec(NSA_Q_W, COL_Q // NSA_Q_W),
                   row_spec(NSA_KV_W, kvb + 2), row_spec(NSA_KV_W, kvb + 3),
                   row_spec(NSA_KV_W, kvb + 4), row_spec(NSA_KV_W, kvb + 5),
                   row_spec(NSA_Q_W, 0),
                   row_spec(NSA_GROUPS * LANES, col_gates // (NSA_GROUPS * LANES)),
                   row_spec(NSA_Q_W, COL_Z_NSA // NSA_Q_W),
                   pl.BlockSpec((1, NSA_GROUPS, dec, LANES), lambda i, pt: (i, 0, 0, 0))])
    return pl.pallas_call(
        kern,
        grid_spec=pltpu.PrefetchScalarGridSpec(
            num_scalar_prefetch=1,
            grid=(n,),
            in_specs=in_specs,
            out_specs=[row_spec(NSA_Q_W, 0), buf_spec, buf_spec],
        ),
        out_shape=[jax.ShapeDtypeStruct((n * dec, NSA_Q_W), F32),
                   jax.ShapeDtypeStruct((n, wb * NSA_GROUPS, HEAD_DIM), F32),
                   jax.ShapeDtypeStruct((n, wb * NSA_GROUPS, HEAD_DIM), F32)],
        compiler_params=_params("parallel"),
        name="nsa_sample",
    )(page_flat, *([kview] * n_pages), *([vview] * n_pages), bk, bv,
      proj_s, proj_s, proj_s, proj_s, proj_s, o_cmp, proj_s, proj_s, sel)


def _diff_sample_kernel(pt_ref, *refs, ppc, dec, lam_init):
    del pt_ref
    k_pages = refs[:ppc]
    v_pages = refs[ppc:2 * ppc]
    (lq1_ref, lk1_ref, lq2_ref, lk2_ref, q_ref, kn_ref, vn_ref, z_ref, g_ref, o_ref,
     m_sc, l_sc, acc_sc) = refs[2 * ppc:]
    j = pl.program_id(1)
    last = pl.num_programs(1) - 1

    @pl.when(j == 0)
    def _():
        m_sc[...] = jnp.full(m_sc.shape, NEG_INF, F32)
        l_sc[...] = jnp.zeros(l_sc.shape, F32)
        acc_sc[...] = jnp.zeros(acc_sc.shape, F32)

    rows = 2 * dec
    n_past = ppc * PAGE_SIZE
    n_keys = n_past + PAD_ROWS
    ki = lax.broadcasted_iota(jnp.int32, (rows, n_keys), 1)
    t = lax.broadcasted_iota(jnp.int32, (rows, n_keys), 0) & (dec - 1)
    new_gate = jnp.where(j == last, 0, -n_keys)
    mask = (ki < n_past) | (ki - n_past <= t + new_gate)
    q = q_ref[...]

    def head_rows(ref, c, h):
        return ref[0, pl.ds(c * DIFF_HEADS + h, PAGE_SIZE, stride=2 * DIFF_HEADS), :]

    for h in range(DIFF_HEADS):
        vsl = slice(h * DIFF_VDIM, (h + 1) * DIFF_VDIM)
        vh = jnp.concatenate(
            [jnp.concatenate([head_rows(pg, 0, h), head_rows(pg, 1, h)], axis=1).astype(BF16) for pg in v_pages]
            + [_pad_new(vn_ref[:, vsl])], axis=0)
        for c in range(2):
            sl = slice(h * DIFF_VDIM + c * HEAD_DIM, h * DIFF_VDIM + (c + 1) * HEAD_DIM)
            kh = jnp.concatenate([head_rows(pg, c, h).astype(BF16) for pg in k_pages] + [_pad_new(kn_ref[:, sl])],
                                 axis=0)
            qh = jnp.concatenate([q[:, sl], q[:, sl]], axis=0).astype(BF16)
            s = jnp.where(mask, _dot_nt(qh, kh) * SCALE, NEG_INF)
            idx = 2 * h + c
            m_old = m_sc[idx]
            m_new = jnp.maximum(m_old, jnp.max(s, axis=-1, keepdims=True))
            alpha = jnp.exp(m_old - m_new)
            e = jnp.exp(s - m_new[:, 0:1])
            l_sc[idx] = alpha * l_sc[idx] + jnp.sum(e, axis=-1, keepdims=True)
            acc_sc[idx] = jnp.concatenate([alpha, alpha], axis=1) * acc_sc[idx] + _dot(e.astype(BF16), vh)
            m_sc[idx] = m_new

    @pl.when(j == last)
    def _():
        lam = _lambda(lq1_ref, lk1_ref, lq2_ref, lk2_ref, lam_init)
        for h in range(DIFF_HEADS):
            vsl = slice(h * DIFF_VDIM, (h + 1) * DIFF_VDIM)
            o1 = acc_sc[2 * h][0:dec] / l_sc[2 * h][0:dec, 0:1]
            o2 = acc_sc[2 * h + 1][0:dec] / l_sc[2 * h + 1][0:dec, 0:1]
            o_ref[:, vsl] = _diff_finish(o1 - lam * o2, g_ref[...], z_ref[:, vsl], lam_init)


def _diff_cache_view(cache):
    pool = cache.shape[0]
    v = cache.reshape(pool, PAGE_SIZE, DIFF_HEADS, 2, HEAD_DIM).transpose(0, 1, 3, 2, 4)
    return v.reshape(pool, PAGE_SIZE * 2 * DIFF_HEADS, HEAD_DIM)


def _diff_sample(cache_k, cache_v, page_flat, proj_s, lams, diff_g, lam_init, n, dec, n_pages):
    kview = _diff_cache_view(cache_k)
    vview = _diff_cache_view(cache_v)
    n_chunks = 2 if n_pages % 2 == 0 else 1
    ppc = n_pages // n_chunks
    page_rows = PAGE_SIZE * 2 * DIFF_HEADS
    rows = 2 * dec
    kern = functools.partial(_diff_sample_kernel, ppc=ppc, dec=dec, lam_init=lam_init)

    def page_spec(p):
        return pl.BlockSpec((1, page_rows, HEAD_DIM), lambda i, j, pt: (pt[i * n_pages + j * ppc + p], 0, 0))

    def row_spec(col0):
        return pl.BlockSpec((dec, DIFF_W), lambda i, j, pt: (i, col0 // DIFF_W))

    lam_spec = pl.BlockSpec((1, HEAD_DIM), lambda i, j, pt: (0, 0))
    in_specs = ([page_spec(p) for p in range(ppc)] + [page_spec(p) for p in range(ppc)]
                + [lam_spec] * 4
                + [row_spec(COL_DQ), row_spec(COL_DK), row_spec(COL_DV), row_spec(COL_Z_DIFF),
                   pl.BlockSpec((1, DIFF_VDIM), lambda i, j, pt: (0, 0))])
    return pl.pallas_call(
        kern,
        grid_spec=pltpu.PrefetchScalarGridSpec(
            num_scalar_prefetch=1,
            grid=(n, n_chunks),
            in_specs=in_specs,
            out_specs=pl.BlockSpec((dec, DIFF_W), lambda i, j, pt: (i, 0)),
            scratch_shapes=[pltpu.VMEM((2 * DIFF_HEADS, rows, LANES), F32),
                            pltpu.VMEM((2 * DIFF_HEADS, rows, LANES), F32),
                            pltpu.VMEM((2 * DIFF_HEADS, rows, DIFF_VDIM), F32)],
        ),
        out_shape=jax.ShapeDtypeStruct((n * dec, DIFF_W), F32),
        compiler_params=_params("parallel", "arbitrary"),
        name="diff_sample",
    )(page_flat, *([kview] * ppc), *([vview] * ppc), *lams, proj_s, proj_s, proj_s, proj_s, diff_g)


def kernel(x_prompt, x_sample, cache_nsa_cmp_k, cache_nsa_cmp_v, cache_nsa_sel_k, cache_nsa_sel_v,
           state_nsa_win_k, state_nsa_win_v, cache_diff_k, cache_diff_v, page_table,
           norm_g, w_in, cmp_k_w1, cmp_k_pe, cmp_k_w2, cmp_v_w1, cmp_v_pe, cmp_v_w2,
           diff_lq1, diff_lk1, diff_lq2, diff_lk2, diff_norm_g, w_proj_nsa, w_proj_diff, w_out,
           final_norm_g):
    n_p, seq, d_model = x_prompt.shape
    n_s, dec, _ = x_sample.shape
    n_pages = page_table.shape[1]
    past_len = n_pages * PAGE_SIZE
    wb = state_nsa_win_k.shape[2]
    assert w_in.shape[0] == 1, "one layer only"
    assert seq % 256 == 0 and seq // CMP_STRIDE == LANES and seq >= WINDOW
    assert dec == 8 and wb == WINDOW and past_len % SEL_BLOCK == 0 and past_len // CMP_STRIDE == LANES
    assert ((past_len + dec) // CMP_STRIDE) * CMP_STRIDE <= past_len
    lam_init = 0.8 - 0.6 * math.exp(-0.3 * 0)
    col_gb, col_gates, _ = _proj_layout(d_model)

    w_packed = _pack_w_in(w_in[0], d_model)
    w1_kv = jnp.stack([cmp_k_w1[0], cmp_v_w1[0]]).astype(BF16)
    w1b = w1_kv.reshape(2, 2, CMP_STRIDE, HEAD_DIM, CMP_HIDDEN)
    w1ab_kv = jnp.concatenate([w1b[:, 0], w1b[:, 1]], axis=-1)
    pe_kv = jnp.stack([cmp_k_pe[0], cmp_v_pe[0]]).reshape(2, CMP_BLOCK * HEAD_DIM)
    w2_kv = jnp.stack([cmp_k_w2[0], cmp_v_w2[0]]).astype(BF16)
    lams = tuple(a[0].reshape(1, HEAD_DIM) for a in (diff_lq1, diff_lk1, diff_lq2, diff_lk2))
    diff_g = diff_norm_g[0].reshape(1, DIFF_VDIM)
    w_nsa = w_proj_nsa[0].astype(BF16)
    w_diff = w_proj_diff[0].astype(BF16)
    w_o = w_out[0].astype(BF16)
    page_flat = page_table.reshape(-1).astype(jnp.int32)
    pos_kv = _pos_terms(pe_kv, w1_kv)

    def split_rows(proj, n, s):
        kv0 = COL_KV
        seg = lambda i: proj[:, kv0 + i * NSA_KV_W:kv0 + (i + 1) * NSA_KV_W].reshape(1, n, s, NSA_GROUPS, HEAD_DIM)
        dk = proj[:, COL_DK:COL_DK + DIFF_W].reshape(1, n, s, DIFF_HEADS, DIFF_VDIM)
        dv = proj[:, COL_DV:COL_DV + DIFF_W].reshape(1, n, s, DIFF_HEADS, DIFF_VDIM)
        return [seg(i) for i in range(6)], dk, dv

    xp = x_prompt.reshape(n_p * seq, d_model)
    cos_p, sin_p = _rope_tables(jnp.arange(seq))
    proj_p = _project(_rmsnorm(xp, norm_g[0], BF16), w_packed,
                      jnp.tile(cos_p, (n_p, 1)), jnp.tile(sin_p, (n_p, 1)))
    kvc = _compress_prompt(proj_p, w1ab_kv, pos_kv, w2_kv, n_p, seq)
    o_nsa_p = _nsa_prompt(proj_p, kvc, col_gates, n_p, seq)
    o_diff_p = _diff_prompt(proj_p, lams, diff_g, lam_init, n_p, seq)
    merged_p = _mix1(o_nsa_p, w_nsa, o_diff_p, w_diff, proj_p, col_gb)
    y_prompt = _mix2(merged_p, w_o, xp, final_norm_g).reshape(n_p, seq, d_model)
    segs_p, dk_p, dv_p = split_rows(proj_p, n_p, seq)
    keep_p = min(WINDOW, seq)
    segs_p[4] = segs_p[4][:, :, seq - keep_p:]
    segs_p[5] = segs_p[5][:, :, seq - keep_p:]

    xs = x_sample.reshape(n_s * dec, d_model)
    cos_s, sin_s = _rope_tables(past_len + jnp.arange(dec))
    proj_s = _project(_rmsnorm(xs, norm_g[0], BF16), w_packed,
                      jnp.tile(cos_s, (n_s, 1)), jnp.tile(sin_s, (n_s, 1)))
    fs_k = _cmp_partials(cache_nsa_cmp_k[0], page_flat, w1ab_kv[0], n_s, n_pages)
    fs_v = _cmp_partials(cache_nsa_cmp_v[0], page_flat, w1ab_kv[1], n_s, n_pages)
    o_cmp_s, sel_s = _cmp_sample(fs_k, fs_v, pos_kv, w2_kv, proj_s, n_s, dec, past_len)
    o_nsa_s, win_k_s, win_v_s = _nsa_sample(cache_nsa_sel_k[0], cache_nsa_sel_v[0], state_nsa_win_k[0],
                                            state_nsa_win_v[0], page_flat, proj_s, o_cmp_s, sel_s, col_gates,
                                            n_s, dec, n_pages)
    o_diff_s = _diff_sample(cache_diff_k[0], cache_diff_v[0], page_flat, proj_s, lams, diff_g, lam_init,
                            n_s, dec, n_pages)
    merged_s = _mix1(o_nsa_s, w_nsa, o_diff_s, w_diff, proj_s, col_gb)
    y_sample = _mix2(merged_s, w_o, xs, final_norm_g).reshape(n_s, dec, d_model)
    segs_s, dk_s, dv_s = split_rows(proj_s, n_s, dec)
    segs_s[4] = win_k_s.reshape(1, n_s, wb, NSA_GROUPS, HEAD_DIM)
    segs_s[5] = win_v_s.reshape(1, n_s, wb, NSA_GROUPS, HEAD_DIM)

    return (y_prompt, y_sample, *segs_p, dk_p, dv_p, *segs_s, dk_s, dv_s)
```

```python
import functools
import math

import jax
import jax.numpy as jnp
from jax import lax
from jax.experimental import pallas as pl
from jax.experimental.pallas import tpu as pltpu

F32 = jnp.float32
BF16 = jnp.bfloat16

HEAD_DIM = 128
NSA_GROUPS = 4
NSA_REP = 4
NSA_HEADS = NSA_GROUPS * NSA_REP
CMP_BLOCK = 32
CMP_STRIDE = 16
CMP_HIDDEN = 2 * HEAD_DIM
SEL_BLOCK = 64
N_SELECT = 16
WINDOW = 512
DIFF_HEADS = 8
DIFF_VDIM = 2 * HEAD_DIM
ROPE_THETA = 10000.0
NORM_EPS = 1e-6
FORCE_SCORE = 1e4
NEG_INF = -1e30
SCALE = HEAD_DIM ** -0.5
PAGE_SIZE = 128

LANES = 128
NSA_Q_W = NSA_HEADS * HEAD_DIM
NSA_KV_W = NSA_GROUPS * HEAD_DIM
DIFF_W = DIFF_HEADS * DIFF_VDIM
PAD_ROWS = 128

COL_Q = 0
COL_Z_NSA = COL_Q + NSA_Q_W
COL_DQ = COL_Z_NSA + NSA_Q_W
COL_DK = COL_DQ + DIFF_W
COL_DV = COL_DK + DIFF_W
COL_Z_DIFF = COL_DV + DIFF_W
COL_KV = COL_Z_DIFF + DIFF_W
COL_GA = COL_KV + 6 * NSA_KV_W
PROJ_TN = 512
VMEM_LIMIT = 56 * 1024 * 1024


def _sigmoid(x):
    return 1.0 / (1.0 + jnp.exp(-x))


def _dot(a, b):
    return jnp.dot(a, b, preferred_element_type=F32)


def _dot_nt(a, b):
    return lax.dot_general(a, b, (((1,), (1,)), ((), ())), preferred_element_type=F32)


def _params(*sem):
    return pltpu.CompilerParams(dimension_semantics=sem, vmem_limit_bytes=VMEM_LIMIT)


def _rmsnorm_kernel(x_ref, g_ref, o_ref):
    x = x_ref[...]
    ms = jnp.mean(x * x, axis=-1, keepdims=True)
    o_ref[...] = (x * lax.rsqrt(ms + NORM_EPS) * g_ref[...]).astype(o_ref.dtype)


def _rmsnorm(x2d, g, out_dtype):
    m, d = x2d.shape
    tm = min(256, m)
    return pl.pallas_call(
        _rmsnorm_kernel,
        grid=(m // tm,),
        in_specs=[pl.BlockSpec((tm, d), lambda i: (i, 0)), pl.BlockSpec((1, d), lambda i: (0, 0))],
        out_specs=pl.BlockSpec((tm, d), lambda i: (i, 0)),
        out_shape=jax.ShapeDtypeStruct((m, d), out_dtype),
        compiler_params=_params("parallel"),
        name="rmsnorm",
    )(x2d, g.reshape(1, d))


def _proj_layout(d_model):
    col_gb = COL_GA + d_model
    col_gates = col_gb + d_model
    n_cols = col_gates + PROJ_TN
    return col_gb, col_gates, n_cols


def _pack_w_in(w, d_model):
    splits = (NSA_Q_W, 6 * NSA_KV_W, 3 * NSA_HEADS, NSA_Q_W, DIFF_W, DIFF_W, DIFF_W, DIFF_W, d_model, d_model)
    cuts = [sum(splits[:i + 1]) for i in range(len(splits) - 1)]
    q, kv, gates, z_nsa, dq, dk, dv, z_diff, g_a, g_b = jnp.split(w, cuts, axis=1)
    gates = gates.reshape(w.shape[0], NSA_GROUPS, NSA_REP * 3)
    gates = jnp.pad(gates, ((0, 0), (0, 0), (0, LANES - NSA_REP * 3))).reshape(w.shape[0], NSA_GROUPS * LANES)
    return jnp.concatenate([q, z_nsa, dq, dk, dv, z_diff, kv, g_a, g_b, gates], axis=1).astype(BF16)


def _proj_kernel(h_ref, w_ref, cos_ref, sin_ref, o_ref, *, rope_blocks, silu_blocks, sig_start):
    j = pl.program_id(1)
    o_ref[...] = _dot(h_ref[...], w_ref[...])

    def _in(blocks):
        pred = None
        for lo, hi in blocks:
            c = (j >= lo) & (j < hi)
            pred = c if pred is None else (pred | c)
        return pred

    @pl.when(_in(rope_blocks))
    def _():
        cos = cos_ref[...]
        sin = sin_ref[...]
        for c in range(o_ref.shape[1] // HEAD_DIM):
            sl = slice(c * HEAD_DIM, (c + 1) * HEAD_DIM)
            x = o_ref[:, sl]
            o_ref[:, sl] = x * cos + pltpu.roll(x, HEAD_DIM // 2, 1) * sin

    @pl.when(_in(silu_blocks))
    def _():
        x = o_ref[...]
        o_ref[...] = x * _sigmoid(x)

    @pl.when(j >= sig_start)
    def _():
        o_ref[...] = _sigmoid(o_ref[...])


def _project(h, w_packed, cos, sin):
    m, d = h.shape
    n_cols = w_packed.shape[1]
    tm = min(1024, m)
    tn = PROJ_TN
    b = lambda col: col // tn
    kv0 = b(COL_KV)
    rope_blocks = ((b(COL_Q), b(COL_Z_NSA)), (b(COL_DQ), b(COL_DV)),
                   (kv0, kv0 + 1), (kv0 + 2, kv0 + 3), (kv0 + 4, kv0 + 5))
    silu_blocks = ((b(COL_Z_NSA), b(COL_DQ)), (b(COL_Z_DIFF), b(COL_KV)))
    kern = functools.partial(_proj_kernel, rope_blocks=rope_blocks, silu_blocks=silu_blocks, sig_start=b(COL_GA))
    return pl.pallas_call(
        kern,
        grid=(m // tm, n_cols // tn),
        in_specs=[pl.BlockSpec((tm, d), lambda i, j: (i, 0)),
                  pl.BlockSpec((d, tn), lambda i, j: (0, j)),
                  pl.BlockSpec((tm, HEAD_DIM), lambda i, j: (i, 0)),
                  pl.BlockSpec((tm, HEAD_DIM), lambda i, j: (i, 0))],
        out_specs=pl.BlockSpec((tm, tn), lambda i, j: (i, j)),
        out_shape=jax.ShapeDtypeStruct((m, n_cols), F32),
        compiler_params=_params("parallel", "arbitrary"),
        name="in_proj",
    )(h, w_packed, cos, sin)


def _rope_tables(pos):
    half = HEAD_DIM // 2
    inv = ROPE_THETA ** (-jnp.arange(half, dtype=F32) / half)
    ang = pos.astype(F32)[:, None] * inv[None, :]
    cos, sin = jnp.cos(ang), jnp.sin(ang)
    return jnp.concatenate([cos, cos], axis=1), jnp.concatenate([-sin, sin], axis=1)


def _pos_term_kernel(pe_ref, w1_ref, o_ref):
    pe = jnp.broadcast_to(pe_ref[0], (8, pe_ref.shape[2])).astype(BF16)
    o_ref[0] = _dot(pe, w1_ref[0])


def _pos_terms(pe_kv, w1_kv):
    kdim = pe_kv.shape[1]
    out = pl.pallas_call(
        _pos_term_kernel,
        grid=(2,),
        in_specs=[pl.BlockSpec((1, 1, kdim), lambda i: (i, 0, 0)),
                  pl.BlockSpec((1, kdim, CMP_HIDDEN), lambda i: (i, 0, 0))],
        out_specs=pl.BlockSpec((1, 8, CMP_HIDDEN), lambda i: (i, 0, 0)),
        out_shape=jax.ShapeDtypeStruct((2, 8, CMP_HIDDEN), F32),
        compiler_params=_params("arbitrary"),
        name="cmp_pos_term",
    )(pe_kv.reshape(2, 1, kdim), w1_kv)
    return out[:, 0:1, :]


def _finish_compress(first, second, pos, w2):
    rows = first.shape[0]
    shifted = pltpu.roll(second, rows - 1, 0)
    pre = first + shifted + pos
    hid = pre * _sigmoid(pre)
    return _dot(hid.astype(BF16), w2)


def _compress_prompt_kernel(x0_ref, x1_ref, x2_ref, x3_ref, w1_ref, pos_ref, w2_ref, o_ref, *, nh):
    x_refs = (x0_ref, x1_ref, x2_ref, x3_ref)
    acc = jnp.zeros((NSA_GROUPS * nh, 2 * CMP_HIDDEN), F32)
    for s in range(CMP_STRIDE):
        xs = jnp.concatenate([x[pl.ds(s, nh, stride=CMP_STRIDE), :] for x in x_refs], axis=0)
        acc = acc + _dot(xs.astype(BF16), w1_ref[0, s])
    out = _finish_compress(acc[:, :CMP_HIDDEN], acc[:, CMP_HIDDEN:], pos_ref[0], w2_ref[0])
    o_ref[0, 0] = out.reshape(NSA_GROUPS, nh, HEAD_DIM)


def _compress_prompt(proj, w1ab_kv, pos_kv, w2_kv, n, seq):
    nh = seq // CMP_STRIDE
    cb = COL_KV // HEAD_DIM
    kern = functools.partial(_compress_prompt_kernel, nh=nh)

    def group_spec(g):
        return pl.BlockSpec((seq, HEAD_DIM), lambda c, i: (i, cb + c * NSA_GROUPS + g))

    return pl.pallas_call(
        kern,
        grid=(2, n),
        in_specs=[group_spec(g) for g in range(NSA_GROUPS)]
        + [pl.BlockSpec((1, CMP_STRIDE, HEAD_DIM, 2 * CMP_HIDDEN), lambda c, i: (c, 0, 0, 0)),
           pl.BlockSpec((1, 1, CMP_HIDDEN), lambda c, i: (c, 0, 0)),
           pl.BlockSpec((1, CMP_HIDDEN, HEAD_DIM), lambda c, i: (c, 0, 0))],
        out_specs=pl.BlockSpec((1, 1, NSA_GROUPS, nh, HEAD_DIM), lambda c, i: (c, i, 0, 0, 0)),
        out_shape=jax.ShapeDtypeStruct((2, n, NSA_GROUPS, nh, HEAD_DIM), F32),
        compiler_params=_params("arbitrary", "arbitrary"),
        name="compress_prompt",
    )(proj, proj, proj, proj, w1ab_kv, pos_kv, w2_kv)


QK_SCALE = SCALE * math.log2(math.e)


def _scaled_q(q):
    return (q * QK_SCALE).astype(BF16)


def _cmp_probs(qs, kc, trow, nc):
    s = _dot_nt(qs, kc)
    cidx = lax.broadcasted_iota(jnp.int32, s.shape, 1)
    vmask = ((cidx * CMP_STRIDE + CMP_BLOCK) <= trow + 1) & (cidx < nc)
    sm = jnp.where(vmask, s, NEG_INF)
    e = jnp.exp2(sm - jnp.max(sm, axis=-1, keepdims=True))
    p = e / jnp.sum(e, axis=-1, keepdims=True)
    return jnp.where(vmask, p, 0.0)


def _block_scores(psum, nblk):
    nc_rows = psum.shape[1]
    cc = lax.broadcasted_iota(jnp.int32, (nc_rows, LANES), 0) * CMP_STRIDE
    jj = lax.broadcasted_iota(jnp.int32, (nc_rows, LANES), 1)
    cover = (cc < jj * SEL_BLOCK + SEL_BLOCK) & (cc + CMP_BLOCK > jj * SEL_BLOCK) & (jj < nblk)
    cover = jnp.where(cover, 1.0, 0.0).astype(BF16)
    hi = psum.astype(BF16)
    r1 = psum - hi.astype(F32)
    mid = r1.astype(BF16)
    lo = (r1 - mid.astype(F32)).astype(BF16)
    return _dot(hi, cover) + _dot(mid, cover) + _dot(lo, cover)


def _rank_select(ranked, nblk, nsel):
    rt = ranked.T[0:nblk]
    ji = lax.broadcasted_iota(jnp.int32, rt.shape, 0)
    rank = jnp.zeros(rt.shape, F32)
    for i in range(nblk):
        row = rt[i:i + 1, :]
        beats = (row > rt) | ((row == rt) & (ji > i))
        rank = rank + jnp.where(beats, 1.0, 0.0)
    sel_t = jnp.where(rank < nsel, 1.0, 0.0)
    sel_t = jnp.concatenate([sel_t, jnp.zeros((LANES - nblk, rt.shape[1]), F32)], axis=0)
    return sel_t.T


def _expand_blocks(sel01, n_keys):
    jj = lax.broadcasted_iota(jnp.int32, (LANES, n_keys), 0)
    kk = lax.broadcasted_iota(jnp.int32, (LANES, n_keys), 1)
    expand = jnp.where((kk // SEL_BLOCK) == jj, 1.0, 0.0).astype(BF16)
    return _dot(sel01, expand)


def _softmax_weights(s):
    e = jnp.exp2(s - jnp.max(s, axis=-1, keepdims=True))
    return e, jnp.sum(e, axis=-1, keepdims=True)


def _masked_attend(qs, k, v, mask):
    e, l = _softmax_weights(jnp.where(mask, _dot_nt(qs, k), NEG_INF))
    return _dot(e.astype(BF16), v) / l


def _biased_attend(qs, k, v, bias):
    e, l = _softmax_weights(_dot_nt(qs, k) + bias)
    return _dot(e.astype(BF16), v) / l


def _online_update(m, l, acc, s, v):
    m_new = jnp.maximum(m, jnp.max(s, axis=-1, keepdims=True))
    alpha = jnp.exp2(m - m_new)
    e = jnp.exp2(s - m_new)
    return m_new, alpha * l + jnp.sum(e, axis=-1, keepdims=True), alpha * acc + _dot(e.astype(BF16), v)


def _nsa_prompt_kernel(q_ref, kc_ref, vc_ref, ks_ref, vs_ref, kw_ref, vw_ref, gate_ref, z_ref, o_ref,
                       ksb, vsb, kwb, vwb, osel, *, tq, seq, nc, nblk, nsel, kstep):
    qb = pl.program_id(2)
    q0 = qb * tq

    @pl.when(qb == 0)
    def _():
        ksb[...] = ks_ref[...].astype(BF16)
        vsb[...] = vs_ref[...].astype(BF16)
        kwb[...] = kw_ref[...].astype(BF16)
        vwb[...] = vw_ref[...].astype(BF16)

    q = q_ref[...]
    qs = _scaled_q(jnp.concatenate([q[:, r * HEAD_DIM:(r + 1) * HEAD_DIM] for r in range(NSA_REP)], axis=0))

    kc = kc_ref[0, 0, 0].astype(BF16)
    vc = vc_ref[0, 0, 0].astype(BF16)
    ncp = kc.shape[0]
    trow = (lax.broadcasted_iota(jnp.int32, (NSA_REP * tq, ncp), 0) & (tq - 1)) + q0
    p = _cmp_probs(qs, kc, trow, nc)
    o_cmp = _dot(p.astype(BF16), vc)

    psum = p[0:tq] + p[tq:2 * tq] + p[2 * tq:3 * tq] + p[3 * tq:4 * tq]
    score = _block_scores(psum, nblk)
    jl = lax.broadcasted_iota(jnp.int32, (tq, LANES), 1)
    tl = lax.broadcasted_iota(jnp.int32, (tq, LANES), 0) + q0
    cur = tl // SEL_BLOCK
    forced = (jl == 0) | (jl == cur) | (jl == cur - 1)
    ranked = jnp.where(forced, FORCE_SCORE, jnp.where(jl <= cur, score, -1.0))
    sel01 = jnp.where(jl <= cur, _rank_select(ranked, nblk, nsel), 0.0).astype(BF16)

    for v in range(seq // kstep):
        ext = (v + 1) * kstep

        @pl.when(q0 // kstep == v)
        def _(ext=ext):
            key_sel = _expand_blocks(sel01, ext)
            kpos = lax.broadcasted_iota(jnp.int32, (tq, ext), 1)
            tpos = lax.broadcasted_iota(jnp.int32, (tq, ext), 0) + q0
            bias = jnp.where((key_sel > 0.5) & (kpos <= tpos), 0.0, NEG_INF)
            ks = ksb[0:ext, :]
            vs = vsb[0:ext, :]
            for r in range(NSA_REP):
                osel[r * tq:(r + 1) * tq, :] = _biased_attend(qs[r * tq:(r + 1) * tq], ks, vs, bias)

    span = min(tq + WINDOW, seq)
    start = pl.multiple_of(jnp.clip(q0 + tq - span, 0, seq - span), tq)
    wk = lax.broadcasted_iota(jnp.int32, (tq, span), 1) + start
    wd = (lax.broadcasted_iota(jnp.int32, (tq, span), 0) + q0) - wk
    win_bias = jnp.where((wd >= 0) & (wd < WINDOW), 0.0, NEG_INF)
    kw = kwb[pl.ds(start, span), :]
    vw = vwb[pl.ds(start, span), :]
    gt = gate_ref[...]
    z = z_ref[...]
    for r in range(NSA_REP):
        o_sel = osel[r * tq:(r + 1) * tq, :]
        o_win = _biased_attend(qs[r * tq:(r + 1) * tq], kw, vw, win_bias)
        o = (gt[:, 3 * r:3 * r + 1] * o_cmp[r * tq:(r + 1) * tq] + gt[:, 3 * r + 1:3 * r + 2] * o_sel
             + gt[:, 3 * r + 2:3 * r + 3] * o_win)
        sl = slice(r * HEAD_DIM, (r + 1) * HEAD_DIM)
        o_ref[:, sl] = (o * z[:, sl]).astype(o_ref.dtype)


def _nsa_prompt(proj, kvc, col_gates, n, seq):
    tq = 128
    nqb = seq // tq
    nh = kvc.shape[3]
    nblk = seq // SEL_BLOCK
    kern = functools.partial(_nsa_prompt_kernel, tq=tq, seq=seq, nc=nh - 1, nblk=nblk, nsel=min(N_SELECT, nblk),
                             kstep=512)
    gw = NSA_REP * HEAD_DIM
    kvb = COL_KV // HEAD_DIM

    def kv_spec(which):
        return pl.BlockSpec((seq, HEAD_DIM), lambda i, g, qb: (i, kvb + which * NSA_GROUPS + g))

    return pl.pallas_call(
        kern,
        grid=(n, NSA_GROUPS, nqb),
        in_specs=[pl.BlockSpec((tq, gw), lambda i, g, qb: (i * nqb + qb, COL_Q // gw + g)),
                  pl.BlockSpec((1, 1, 1, nh, HEAD_DIM), lambda i, g, qb: (0, i, g, 0, 0)),
                  pl.BlockSpec((1, 1, 1, nh, HEAD_DIM), lambda i, g, qb: (1, i, g, 0, 0)),
                  kv_spec(2), kv_spec(3), kv_spec(4), kv_spec(5),
                  pl.BlockSpec((tq, LANES), lambda i, g, qb: (i * nqb + qb, col_gates // LANES + g)),
                  pl.BlockSpec((tq, gw), lambda i, g, qb: (i * nqb + qb, COL_Z_NSA // gw + g))],
        out_specs=pl.BlockSpec((tq, gw), lambda i, g, qb: (i * nqb + qb, g)),
        out_shape=jax.ShapeDtypeStruct((n * seq, NSA_Q_W), BF16),
        scratch_shapes=[pltpu.VMEM((seq, HEAD_DIM), BF16)] * 4 + [pltpu.VMEM((NSA_REP * tq, HEAD_DIM), F32)],
        compiler_params=_params("arbitrary", "arbitrary", "arbitrary"),
        name="nsa_prompt",
    )(proj, kvc, kvc, proj, proj, proj, proj, proj, proj)


def _lambda(lq1_ref, lk1_ref, lq2_ref, lk2_ref, lam_init):
    a = jnp.sum(lq1_ref[...] * lk1_ref[...], axis=-1, keepdims=True)
    b = jnp.sum(lq2_ref[...] * lk2_ref[...], axis=-1, keepdims=True)
    return jnp.exp(a) - jnp.exp(b) + lam_init


def _diff_finish(o, g, z, lam_init):
    ms = jnp.mean(o * o, axis=-1, keepdims=True)
    return o * lax.rsqrt(ms + NORM_EPS) * g * (1.0 - lam_init) * z


def _diff_prompt_kernel(lq1_ref, lk1_ref, lq2_ref, lk2_ref, q_ref, k_ref, v_ref, z_ref, g_ref, o_ref,
                        kb, vb, *, tq, seq, lam_init, kstep):
    qb = pl.program_id(2)
    q0 = qb * tq

    @pl.when(qb == 0)
    def _():
        kb[...] = k_ref[...].astype(BF16)
        vb[...] = v_ref[...].astype(BF16)

    lam = _lambda(lq1_ref, lk1_ref, lq2_ref, lk2_ref, lam_init)
    q = _scaled_q(q_ref[...])

    for v in range(seq // kstep):
        ext = (v + 1) * kstep

        @pl.when(q0 // kstep == v)
        def _(ext=ext):
            causal = (lax.broadcasted_iota(jnp.int32, (tq, ext), 1)
                      <= lax.broadcasted_iota(jnp.int32, (tq, ext), 0) + q0)
            bias = jnp.where(causal, 0.0, NEG_INF)
            k = kb[0:ext, :]
            maps = []
            for c in range(2):
                sl = slice(c * HEAD_DIM, (c + 1) * HEAD_DIM)
                e, l = _softmax_weights(_dot_nt(q[:, sl], k[:, sl]) + bias)
                maps.append(e / l)
            w = maps[0] - lam * maps[1]
            o = _dot(w.astype(BF16), vb[0:ext, :])
            o_ref[...] = _diff_finish(o, g_ref[...], z_ref[...], lam_init).astype(o_ref.dtype)


def _diff_prompt(proj, lams, diff_g, lam_init, n, seq):
    tq = 256
    nqb = seq // tq
    kern = functools.partial(_diff_prompt_kernel, tq=tq, seq=seq, lam_init=lam_init, kstep=512)
    hw = DIFF_VDIM
    lam_spec = pl.BlockSpec((1, HEAD_DIM), lambda i, h, qb: (0, 0))
    return pl.pallas_call(
        kern,
        grid=(n, DIFF_HEADS, nqb),
        in_specs=[lam_spec, lam_spec, lam_spec, lam_spec,
                  pl.BlockSpec((tq, hw), lambda i, h, qb: (i * nqb + qb, COL_DQ // hw + h)),
                  pl.BlockSpec((seq, hw), lambda i, h, qb: (i, COL_DK // hw + h)),
                  pl.BlockSpec((seq, hw), lambda i, h, qb: (i, COL_DV // hw + h)),
                  pl.BlockSpec((tq, hw), lambda i, h, qb: (i * nqb + qb, COL_Z_DIFF // hw + h)),
                  pl.BlockSpec((1, hw), lambda i, h, qb: (0, 0))],
        out_specs=pl.BlockSpec((tq, hw), lambda i, h, qb: (i * nqb + qb, h)),
        out_shape=jax.ShapeDtypeStruct((n * seq, DIFF_W), BF16),
        scratch_shapes=[pltpu.VMEM((seq, hw), BF16)] * 2,
        compiler_params=_params("arbitrary", "arbitrary", "arbitrary"),
        name="diff_prompt",
    )(*lams, proj, proj, proj, proj, diff_g)


def _mix1_kernel(a_ref, wa_ref, b_ref, wb_ref, ga_ref, gb_ref, o_ref):
    pa = _dot(a_ref[...].astype(BF16), wa_ref[...])
    pb = _dot(b_ref[...].astype(BF16), wb_ref[...])
    o_ref[...] = (ga_ref[...] * pa + gb_ref[...] * pb).astype(o_ref.dtype)


def _mix1(o_nsa, w_nsa, o_diff, w_diff, proj, col_gb):
    m = o_nsa.shape[0]
    d_model = w_nsa.shape[1]
    tm = min(512, m)
    tn = 512
    return pl.pallas_call(
        _mix1_kernel,
        grid=(m // tm, d_model // tn),
        in_specs=[pl.BlockSpec((tm, NSA_Q_W), lambda i, j: (i, 0)),
                  pl.BlockSpec((NSA_Q_W, tn), lambda i, j: (0, j)),
                  pl.BlockSpec((tm, DIFF_W), lambda i, j: (i, 0)),
                  pl.BlockSpec((DIFF_W, tn), lambda i, j: (0, j)),
                  pl.BlockSpec((tm, tn), lambda i, j: (i, COL_GA // tn + j)),
                  pl.BlockSpec((tm, tn), lambda i, j: (i, col_gb // tn + j))],
        out_specs=pl.BlockSpec((tm, tn), lambda i, j: (i, j)),
        out_shape=jax.ShapeDtypeStruct((m, d_model), BF16),
        compiler_params=_params("parallel", "arbitrary"),
        name="mix_gated_proj",
    )(o_nsa, w_nsa, o_diff, w_diff, proj, proj)


def _mix2_kernel(m_ref, w_ref, x_ref, g_ref, o_ref):
    k = pl.program_id(1)
    part = _dot(m_ref[...], w_ref[...])

    @pl.when(k == 0)
    def _():
        o_ref[...] = part

    @pl.when(k > 0)
    def _():
        o_ref[...] += part

    @pl.when(k == pl.num_programs(1) - 1)
    def _():
        y = x_ref[...] + o_ref[...]
        ms = jnp.mean(y * y, axis=-1, keepdims=True)
        o_ref[...] = y * lax.rsqrt(ms + NORM_EPS) * g_ref[...]


def _mix2(merged, w_out, x2d, final_g):
    m, d_model = x2d.shape
    tm = min(256, m)
    tk = 1024
    return pl.pallas_call(
        _mix2_kernel,
        grid=(m // tm, d_model // tk),
        in_specs=[pl.BlockSpec((tm, tk), lambda i, k: (i, k)),
                  pl.BlockSpec((tk, d_model), lambda i, k: (k, 0)),
                  pl.BlockSpec((tm, d_model), lambda i, k: (i, 0)),
                  pl.BlockSpec((1, d_model), lambda i, k: (0, 0))],
        out_specs=pl.BlockSpec((tm, d_model), lambda i, k: (i, 0)),
        out_shape=jax.ShapeDtypeStruct((m, d_model), F32),
        compiler_params=_params("parallel", "arbitrary"),
        name="out_proj_residual_norm",
    )(merged, w_out, x2d, final_g.reshape(1, d_model))


def _cmp_partial_kernel(pt_ref, *refs, nbs):
    del pt_ref
    x_refs, w_ref, o_ref = refs[:nbs], refs[nbs], refs[nbs + 1]
    hpp = PAGE_SIZE // CMP_STRIDE
    acc = jnp.zeros((nbs * NSA_GROUPS * hpp, 2 * CMP_HIDDEN), F32)
    for s in range(CMP_STRIDE):
        pieces = [x_refs[i][0, pl.ds(NSA_GROUPS * s + g, hpp, stride=NSA_GROUPS * CMP_STRIDE), :]
                  for i in range(nbs) for g in range(NSA_GROUPS)]
        xs = jnp.concatenate(pieces, axis=0).astype(BF16)
        acc = acc + _dot(xs, w_ref[s])
    o_ref[...] = acc.reshape(nbs, NSA_GROUPS, hpp, 2 * CMP_HIDDEN)


def _cmp_partials(cache, page_flat, w1ab, n, n_pages):
    pool = cache.shape[0]
    hpp = PAGE_SIZE // CMP_STRIDE
    view = cache.reshape(pool, PAGE_SIZE * NSA_GROUPS, HEAD_DIM)
    nbs = min(16, n)
    kern = functools.partial(_cmp_partial_kernel, nbs=nbs)

    def page_spec(i):
        return pl.BlockSpec((1, PAGE_SIZE * NSA_GROUPS, HEAD_DIM),
                            lambda nb, p, pt: (pt[(nb * nbs + i) * n_pages + p], 0, 0))

    return pl.pallas_call(
        kern,
        grid_spec=pltpu.PrefetchScalarGridSpec(
            num_scalar_prefetch=1,
            grid=(n // nbs, n_pages),
            in_specs=[page_spec(i) for i in range(nbs)]
            + [pl.BlockSpec((CMP_STRIDE, HEAD_DIM, 2 * CMP_HIDDEN), lambda nb, p, pt: (0, 0, 0))],
            out_specs=pl.BlockSpec((nbs, NSA_GROUPS, hpp, 2 * CMP_HIDDEN), lambda nb, p, pt: (nb, 0, p, 0)),
        ),
        out_shape=jax.ShapeDtypeStruct((n, NSA_GROUPS, n_pages * hpp, 2 * CMP_HIDDEN), F32),
        compiler_params=_params("parallel", "arbitrary"),
        name="cmp_partials_sample",
    )(page_flat, *([view] * nbs), w1ab)


def _cmp_sample_kernel(fsk_ref, fsv_ref, pos_ref, w2_ref, q_ref, ocmp_ref, sel_ref,
                       *, nb2, nh, nc, past_len, dec, nblk, nsel):
    rows = nb2 * NSA_GROUPS * nh

    def compress(fs_ref, c):
        fs = fs_ref[...].reshape(rows, 2 * CMP_HIDDEN)
        return _finish_compress(fs[:, :CMP_HIDDEN], fs[:, CMP_HIDDEN:], pos_ref[c], w2_ref[c])

    kc = compress(fsk_ref, 0).astype(BF16)
    vc = compress(fsv_ref, 1).astype(BF16)
    trow = (lax.broadcasted_iota(jnp.int32, (NSA_REP * dec, nh), 0) & (dec - 1)) + past_len
    scores = []
    for i in range(nb2):
        qi = q_ref[i * dec:(i + 1) * dec, :]
        for g in range(NSA_GROUPS):
            base = (i * NSA_GROUPS + g) * nh
            c0 = g * NSA_REP * HEAD_DIM
            qg = _scaled_q(jnp.concatenate([qi[:, c0 + r * HEAD_DIM:c0 + (r + 1) * HEAD_DIM] for r in range(NSA_REP)],
                                           axis=0))
            p = _cmp_probs(qg, kc[base:base + nh], trow, nc)
            o = _dot(p.astype(BF16), vc[base:base + nh])
            for r in range(NSA_REP):
                ocmp_ref[i * dec:(i + 1) * dec, c0 + r * HEAD_DIM:c0 + (r + 1) * HEAD_DIM] = o[r * dec:(r + 1) * dec]
            psum = p[0:dec] + p[dec:2 * dec] + p[2 * dec:3 * dec] + p[3 * dec:4 * dec]
            scores.append(_block_scores(psum, nblk))
    score = jnp.concatenate(scores, axis=0)
    jl = lax.broadcasted_iota(jnp.int32, score.shape, 1)
    ranked = jnp.where((jl == 0) | (jl == nblk - 1), FORCE_SCORE, score)
    sel_ref[...] = _rank_select(ranked, nblk, nsel).reshape(nb2, NSA_GROUPS, dec, LANES)


def _cmp_sample(fs_k, fs_v, pos_kv, w2_kv, proj_s, n, dec, past_len):
    nh = fs_k.shape[2]
    nb2 = min(4, n)
    nblk = past_len // SEL_BLOCK
    kern = functools.partial(_cmp_sample_kernel, nb2=nb2, nh=nh, nc=nh - 1, past_len=past_len, dec=dec,
                             nblk=nblk, nsel=min(N_SELECT - 1, nblk))
    fs_spec = pl.BlockSpec((nb2, NSA_GROUPS, nh, 2 * CMP_HIDDEN), lambda i: (i, 0, 0, 0))
    return pl.pallas_call(
        kern,
        grid=(n // nb2,),
        in_specs=[fs_spec, fs_spec,
                  pl.BlockSpec((2, 1, CMP_HIDDEN), lambda i: (0, 0, 0)),
                  pl.BlockSpec((2, CMP_HIDDEN, HEAD_DIM), lambda i: (0, 0, 0)),
                  pl.BlockSpec((nb2 * dec, NSA_Q_W), lambda i: (i, 0))],
        out_specs=[pl.BlockSpec((nb2 * dec, NSA_Q_W), lambda i: (i, 0)),
                   pl.BlockSpec((nb2, NSA_GROUPS, dec, LANES), lambda i: (i, 0, 0, 0))],
        out_shape=[jax.ShapeDtypeStruct((n * dec, NSA_Q_W), F32),
                   jax.ShapeDtypeStruct((n, NSA_GROUPS, dec, LANES), F32)],
        compiler_params=_params("parallel"),
        name="cmp_attend_select_sample",
    )(fs_k, fs_v, pos_kv, w2_kv, proj_s)


def _pad_new(x):
    return jnp.concatenate([x, jnp.zeros((PAD_ROWS - x.shape[0], x.shape[1]), x.dtype)], axis=0).astype(BF16)


def _nsa_sample_kernel(pt_ref, *refs, n_pages, past_len, dec, wb):
    del pt_ref
    ksel_pages = refs[:n_pages]
    vsel_pages = refs[n_pages:2 * n_pages]
    (bufk_ref, bufv_ref, q_ref, ksn_ref, vsn_ref, kwn_ref, vwn_ref, ocmp_ref, gate_ref, z_ref, sel_ref,
     o_ref, nbk_ref, nbv_ref) = refs[2 * n_pages:]
    rows = NSA_REP * dec
    t_of_row = lambda shape: lax.broadcasted_iota(jnp.int32, shape, 0) & (dec - 1)

    def group_rows(ref, g, n_tok):
        return ref[0, pl.ds(g, n_tok, stride=NSA_GROUPS), :]

    keep_rows = (wb - dec) * NSA_GROUPS
    for buf_ref, new_ref, out_ref in ((bufk_ref, kwn_ref, nbk_ref), (bufv_ref, vwn_ref, nbv_ref)):
        out_ref[0, 0:keep_rows, :] = buf_ref[0, dec * NSA_GROUPS:wb * NSA_GROUPS, :]
        for g in range(NSA_GROUPS):
            out_ref[0, pl.ds(keep_rows + g, dec, stride=NSA_GROUPS), :] = new_ref[:, g * HEAD_DIM:(g + 1) * HEAD_DIM]

    jn = lax.broadcasted_iota(jnp.int32, (rows, PAD_ROWS), 1)
    new_ok = jnp.where((jn < dec) & (jn <= t_of_row((rows, PAD_ROWS))), 1.0, 0.0)
    n_win = wb + PAD_ROWS
    wi = lax.broadcasted_iota(jnp.int32, (rows, n_win), 1)
    wkpos = jnp.where(wi < wb, past_len - wb + wi, past_len + wi - wb)
    wd = (past_len + t_of_row((rows, n_win))) - wkpos
    win_mask = (wd >= 0) & (wd < WINDOW) & (wi < wb + dec)

    q = q_ref[...]
    gt = gate_ref[...]
    z = z_ref[...]
    for g in range(NSA_GROUPS):
        sl = slice(g * HEAD_DIM, (g + 1) * HEAD_DIM)
        c0 = g * NSA_REP * HEAD_DIM
        qg = _scaled_q(jnp.concatenate([q[:, c0 + r * HEAD_DIM:c0 + (r + 1) * HEAD_DIM] for r in range(NSA_REP)],
                                       axis=0))
        kg = jnp.concatenate([group_rows(pg, g, PAGE_SIZE).astype(BF16) for pg in ksel_pages]
                             + [_pad_new(ksn_ref[:, sl])], axis=0)
        vg = jnp.concatenate([group_rows(pg, g, PAGE_SIZE).astype(BF16) for pg in vsel_pages]
                             + [_pad_new(vsn_ref[:, sl])], axis=0)
        key_sel = _expand_blocks(sel_ref[0, g].astype(BF16), past_len)
        ok = jnp.concatenate([jnp.concatenate([key_sel] * NSA_REP, axis=0), new_ok], axis=1)
        o_sel = _masked_attend(qg, kg, vg, ok > 0.5)
        kw = jnp.concatenate([group_rows(bufk_ref, g, wb).astype(BF16), _pad_new(kwn_ref[:, sl])], axis=0)
        vw = jnp.concatenate([group_rows(bufv_ref, g, wb).astype(BF16), _pad_new(vwn_ref[:, sl])], axis=0)
        o_win = _masked_attend(qg, kw, vw, win_mask)
        for r in range(NSA_REP):
            cs = slice(c0 + r * HEAD_DIM, c0 + (r + 1) * HEAD_DIM)
            rs = slice(r * dec, (r + 1) * dec)
            gi = g * LANES + 3 * r
            o = (gt[:, gi:gi + 1] * ocmp_ref[:, cs] + gt[:, gi + 1:gi + 2] * o_sel[rs]
                 + gt[:, gi + 2:gi + 3] * o_win[rs])
            o_ref[:, cs] = o * z[:, cs]


def _nsa_sample(cache_k, cache_v, buf_k, buf_v, page_flat, proj_s, o_cmp, sel, col_gates, n, dec, n_pages):
    pool = cache_k.shape[0]
    wb = buf_k.shape[1]
    past_len = n_pages * PAGE_SIZE
    kview = cache_k.reshape(pool, PAGE_SIZE * NSA_GROUPS, HEAD_DIM)
    vview = cache_v.reshape(pool, PAGE_SIZE * NSA_GROUPS, HEAD_DIM)
    bk = buf_k.reshape(n, wb * NSA_GROUPS, HEAD_DIM)
    bv = buf_v.reshape(n, wb * NSA_GROUPS, HEAD_DIM)
    kern = functools.partial(_nsa_sample_kernel, n_pages=n_pages, past_len=past_len, dec=dec, wb=wb)
    kvb = COL_KV // NSA_KV_W

    def page_spec(p):
        return pl.BlockSpec((1, PAGE_SIZE * NSA_GROUPS, HEAD_DIM), lambda i, pt: (pt[i * n_pages + p], 0, 0))

    def row_spec(width, col_block):
        return pl.BlockSpec((dec, width), lambda i, pt: (i, col_block))

    buf_spec = pl.BlockSpec((1, wb * NSA_GROUPS, HEAD_DIM), lambda i, pt: (i, 0, 0))
    in_specs = ([page_spec(p) for p in range(n_pages)] + [page_spec(p) for p in range(n_pages)]
                + [buf_spec, buf_spec,
                   row_spec(NSA_Q_W, COL_Q // NSA_Q_W),
                   row_spec(NSA_KV_W, kvb + 2), row_spec(NSA_KV_W, kvb + 3),
                   row_spec(NSA_KV_W, kvb + 4), row_spec(NSA_KV_W, kvb + 5),
                   row_spec(NSA_Q_W, 0),
                   row_spec(NSA_GROUPS * LANES, col_gates // (NSA_GROUPS * LANES)),
                   row_spec(NSA_Q_W, COL_Z_NSA // NSA_Q_W),
                   pl.BlockSpec((1, NSA_GROUPS, dec, LANES), lambda i, pt: (i, 0, 0, 0))])
    return pl.pallas_call(
        kern,
        grid_spec=pltpu.PrefetchScalarGridSpec(
            num_scalar_prefetch=1,
            grid=(n,),
            in_specs=in_specs,
            out_specs=[row_spec(NSA_Q_W, 0), buf_spec, buf_spec],
        ),
        out_shape=[jax.ShapeDtypeStruct((n * dec, NSA_Q_W), F32),
                   jax.ShapeDtypeStruct((n, wb * NSA_GROUPS, HEAD_DIM), F32),
                   jax.ShapeDtypeStruct((n, wb * NSA_GROUPS, HEAD_DIM), F32)],
        compiler_params=_params("parallel"),
        name="nsa_sample",
    )(page_flat, *([kview] * n_pages), *([vview] * n_pages), bk, bv,
      proj_s, proj_s, proj_s, proj_s, proj_s, o_cmp, proj_s, proj_s, sel)


def _diff_sample_kernel(pt_ref, *refs, ppc, dec, lam_init):
    del pt_ref
    k_pages = refs[:ppc]
    v_pages = refs[ppc:2 * ppc]
    (lq1_ref, lk1_ref, lq2_ref, lk2_ref, q_ref, kn_ref, vn_ref, z_ref, g_ref, o_ref,
     m_sc, l_sc, acc_sc) = refs[2 * ppc:]
    j = pl.program_id(1)
    last = pl.num_programs(1) - 1

    @pl.when(j == 0)
    def _():
        m_sc[...] = jnp.full(m_sc.shape, NEG_INF, F32)
        l_sc[...] = jnp.zeros(l_sc.shape, F32)
        acc_sc[...] = jnp.zeros(acc_sc.shape, F32)

    hh = 2 * DIFF_HEADS
    nrow = hh * dec
    page_rows = PAGE_SIZE * hh

    def rows_chd(x, swap=False):
        return jnp.concatenate(
            [x[:, h * DIFF_VDIM + (c ^ swap) * HEAD_DIM:h * DIFF_VDIM + ((c ^ swap) + 1) * HEAD_DIM]
             for c in range(2) for h in range(DIFF_HEADS)], axis=0)

    qall = _scaled_q(rows_chd(q_ref[...]))
    ri = lax.broadcasted_iota(jnp.int32, (nrow, page_rows), 0)
    li = lax.broadcasted_iota(jnp.int32, (nrow, page_rows), 1)
    past_ok = (li & (hh - 1)) == (ri // dec)

    m = m_sc[:, 0:1]
    l = l_sc[:, 0:1]
    acc = acc_sc[...]
    for kp, vp in zip(k_pages, v_pages):
        xb = kp[0].reshape(page_rows, HEAD_DIM).astype(BF16)
        s = jnp.where(past_ok, _dot_nt(qall, xb), NEG_INF)
        v_same = vp[0].reshape(page_rows, HEAD_DIM)
        v_swap = jnp.concatenate([vp[0, :, DIFF_HEADS:hh, :], vp[0, :, 0:DIFF_HEADS, :]], axis=1)
        w2 = jnp.concatenate([v_same, v_swap.reshape(page_rows, HEAD_DIM)], axis=1).astype(BF16)
        m, l, acc = _online_update(m, l, acc, s, w2)

    rn = lax.broadcasted_iota(jnp.int32, (nrow, nrow), 0)
    ln = lax.broadcasted_iota(jnp.int32, (nrow, nrow), 1)
    gate = jnp.where(j == last, 0, -nrow)
    new_ok = ((ln // dec) == (rn // dec)) & ((ln & (dec - 1)) <= (rn & (dec - 1)) + gate)
    s = jnp.where(new_ok, _dot_nt(qall, rows_chd(kn_ref[...]).astype(BF16)), NEG_INF)
    vn = vn_ref[...]
    w2 = jnp.concatenate([rows_chd(vn), rows_chd(vn, swap=True)], axis=1).astype(BF16)
    m, l, acc = _online_update(m, l, acc, s, w2)
    m_sc[...] = jnp.broadcast_to(m, m_sc.shape)
    l_sc[...] = jnp.broadcast_to(l, l_sc.shape)
    acc_sc[...] = acc

    @pl.when(j == last)
    def _():
        lam = _lambda(lq1_ref, lk1_ref, lq2_ref, lk2_ref, lam_init)
        o = acc_sc[...] / l_sc[:, 0:1]
        half = DIFF_HEADS * dec
        for h in range(DIFF_HEADS):
            vsl = slice(h * DIFF_VDIM, (h + 1) * DIFF_VDIM)
            o1 = o[h * dec:(h + 1) * dec]
            r2 = o[half + h * dec:half + (h + 1) * dec]
            o2 = jnp.concatenate([r2[:, HEAD_DIM:], r2[:, :HEAD_DIM]], axis=1)
            o_ref[:, vsl] = _diff_finish(o1 - lam * o2, g_ref[...], z_ref[:, vsl], lam_init)


def _diff_cache_view(cache):
    pool = cache.shape[0]
    v = cache.reshape(pool, PAGE_SIZE, DIFF_HEADS, 2, HEAD_DIM).transpose(0, 1, 3, 2, 4)
    return v.reshape(pool, PAGE_SIZE, 2 * DIFF_HEADS, HEAD_DIM)


def _diff_sample(cache_k, cache_v, page_flat, proj_s, lams, diff_g, lam_init, n, dec, n_pages):
    kview = _diff_cache_view(cache_k)
    vview = _diff_cache_view(cache_v)
    n_chunks = 2 if n_pages % 2 == 0 else 1
    ppc = n_pages // n_chunks
    rows = 2 * DIFF_HEADS * dec
    kern = functools.partial(_diff_sample_kernel, ppc=ppc, dec=dec, lam_init=lam_init)

    def page_spec(p):
        return pl.BlockSpec((1, PAGE_SIZE, 2 * DIFF_HEADS, HEAD_DIM),
                            lambda i, j, pt: (pt[i * n_pages + j * ppc + p], 0, 0, 0))

    def row_spec(col0):
        return pl.BlockSpec((dec, DIFF_W), lambda i, j, pt: (i, col0 // DIFF_W))

    lam_spec = pl.BlockSpec((1, HEAD_DIM), lambda i, j, pt: (0, 0))
    in_specs = ([page_spec(p) for p in range(ppc)] + [page_spec(p) for p in range(ppc)]
                + [lam_spec] * 4
                + [row_spec(COL_DQ), row_spec(COL_DK), row_spec(COL_DV), row_spec(COL_Z_DIFF),
                   pl.BlockSpec((1, DIFF_VDIM), lambda i, j, pt: (0, 0))])
    return pl.pallas_call(
        kern,
        grid_spec=pltpu.PrefetchScalarGridSpec(
            num_scalar_prefetch=1,
            grid=(n, n_chunks),
            in_specs=in_specs,
            out_specs=pl.BlockSpec((dec, DIFF_W), lambda i, j, pt: (i, 0)),
            scratch_shapes=[pltpu.VMEM((rows, LANES), F32),
                            pltpu.VMEM((rows, LANES), F32),
                            pltpu.VMEM((rows, DIFF_VDIM), F32)],
        ),
        out_shape=jax.ShapeDtypeStruct((n * dec, DIFF_W), F32),
        compiler_params=_params("parallel", "arbitrary"),
        name="diff_sample",
    )(page_flat, *([kview] * ppc), *([vview] * ppc), *lams, proj_s, proj_s, proj_s, proj_s, diff_g)


def kernel(x_prompt, x_sample, cache_nsa_cmp_k, cache_nsa_cmp_v, cache_nsa_sel_k, cache_nsa_sel_v,
           state_nsa_win_k, state_nsa_win_v, cache_diff_k, cache_diff_v, page_table,
           norm_g, w_in, cmp_k_w1, cmp_k_pe, cmp_k_w2, cmp_v_w1, cmp_v_pe, cmp_v_w2,
           diff_lq1, diff_lk1, diff_lq2, diff_lk2, diff_norm_g, w_proj_nsa, w_proj_diff, w_out,
           final_norm_g):
    n_p, seq, d_model = x_prompt.shape
    n_s, dec, _ = x_sample.shape
    n_pages = page_table.shape[1]
    past_len = n_pages * PAGE_SIZE
    wb = state_nsa_win_k.shape[2]
    assert w_in.shape[0] == 1, "one layer only"
    assert seq % 256 == 0 and seq // CMP_STRIDE == LANES and seq >= WINDOW
    assert dec == 8 and wb == WINDOW and past_len % SEL_BLOCK == 0 and past_len // CMP_STRIDE == LANES
    assert ((past_len + dec) // CMP_STRIDE) * CMP_STRIDE <= past_len
    lam_init = 0.8 - 0.6 * math.exp(-0.3 * 0)
    col_gb, col_gates, _ = _proj_layout(d_model)

    w_packed = _pack_w_in(w_in[0], d_model)
    w1_kv = jnp.stack([cmp_k_w1[0], cmp_v_w1[0]]).astype(BF16)
    w1b = w1_kv.reshape(2, 2, CMP_STRIDE, HEAD_DIM, CMP_HIDDEN)
    w1ab_kv = jnp.concatenate([w1b[:, 0], w1b[:, 1]], axis=-1)
    pe_kv = jnp.stack([cmp_k_pe[0], cmp_v_pe[0]]).reshape(2, CMP_BLOCK * HEAD_DIM)
    w2_kv = jnp.stack([cmp_k_w2[0], cmp_v_w2[0]]).astype(BF16)
    lams = tuple(a[0].reshape(1, HEAD_DIM) for a in (diff_lq1, diff_lk1, diff_lq2, diff_lk2))
    diff_g = diff_norm_g[0].reshape(1, DIFF_VDIM)
    w_nsa = w_proj_nsa[0].astype(BF16)
    w_diff = w_proj_diff[0].astype(BF16)
    w_o = w_out[0].astype(BF16)
    page_flat = page_table.reshape(-1).astype(jnp.int32)
    pos_kv = _pos_terms(pe_kv, w1_kv)

    def split_rows(proj, n, s):
        kv0 = COL_KV
        seg = lambda i: proj[:, kv0 + i * NSA_KV_W:kv0 + (i + 1) * NSA_KV_W].reshape(1, n, s, NSA_GROUPS, HEAD_DIM)
        dk = proj[:, COL_DK:COL_DK + DIFF_W].reshape(1, n, s, DIFF_HEADS, DIFF_VDIM)
        dv = proj[:, COL_DV:COL_DV + DIFF_W].reshape(1, n, s, DIFF_HEADS, DIFF_VDIM)
        return [seg(i) for i in range(6)], dk, dv

    xp = x_prompt.reshape(n_p * seq, d_model)
    cos_p, sin_p = _rope_tables(jnp.arange(seq))
    proj_p = _project(_rmsnorm(xp, norm_g[0], BF16), w_packed,
                      jnp.tile(cos_p, (n_p, 1)), jnp.tile(sin_p, (n_p, 1)))
    kvc = _compress_prompt(proj_p, w1ab_kv, pos_kv, w2_kv, n_p, seq)
    o_nsa_p = _nsa_prompt(proj_p, kvc, col_gates, n_p, seq)
    o_diff_p = _diff_prompt(proj_p, lams, diff_g, lam_init, n_p, seq)
    merged_p = _mix1(o_nsa_p, w_nsa, o_diff_p, w_diff, proj_p, col_gb)
    y_prompt = _mix2(merged_p, w_o, xp, final_norm_g).reshape(n_p, seq, d_model)
    segs_p, dk_p, dv_p = split_rows(proj_p, n_p, seq)
    keep_p = min(WINDOW, seq)
    segs_p[4] = segs_p[4][:, :, seq - keep_p:]
    segs_p[5] = segs_p[5][:, :, seq - keep_p:]

    xs = x_sample.reshape(n_s * dec, d_model)
    cos_s, sin_s = _rope_tables(past_len + jnp.arange(dec))
    proj_s = _project(_rmsnorm(xs, norm_g[0], BF16), w_packed,
                      jnp.tile(cos_s, (n_s, 1)), jnp.tile(sin_s, (n_s, 1)))
    fs_k = _cmp_partials(cache_nsa_cmp_k[0], page_flat, w1ab_kv[0], n_s, n_pages)
    fs_v = _cmp_partials(cache_nsa_cmp_v[0], page_flat, w1ab_kv[1], n_s, n_pages)
    o_cmp_s, sel_s = _cmp_sample(fs_k, fs_v, pos_kv, w2_kv, proj_s, n_s, dec, past_len)
    o_nsa_s, win_k_s, win_v_s = _nsa_sample(cache_nsa_sel_k[0], cache_nsa_sel_v[0], state_nsa_win_k[0],
                                            state_nsa_win_v[0], page_flat, proj_s, o_cmp_s, sel_s, col_gates,
                                            n_s, dec, n_pages)
    o_diff_s = _diff_sample(cache_diff_k[0], cache_diff_v[0], page_flat, proj_s, lams, diff_g, lam_init,
                            n_s, dec, n_pages)
    merged_s = _mix1(o_nsa_s, w_nsa, o_diff_s, w_diff, proj_s, col_gb)
    y_sample = _mix2(merged_s, w_o, xs, final_norm_g).reshape(n_s, dec, d_model)
    segs_s, dk_s, dv_s = split_rows(proj_s, n_s, dec)
    segs_s[4] = win_k_s.reshape(1, n_s, wb, NSA_GROUPS, HEAD_DIM)
    segs_s[5] = win_v_s.reshape(1, n_s, wb, NSA_GROUPS, HEAD_DIM)

    return (y_prompt, y_sample, *segs_p, dk_p, dv_p, *segs_s, dk_s, dv_s)
```

```python
import functools
import math

import jax
import jax.numpy as jnp
from jax import lax
from jax.experimental import pallas as pl
from jax.experimental.pallas import tpu as pltpu

F32 = jnp.float32
BF16 = jnp.bfloat16

HEAD_DIM = 128
NSA_GROUPS = 4
NSA_REP = 4
NSA_HEADS = NSA_GROUPS * NSA_REP
CMP_BLOCK = 32
CMP_STRIDE = 16
CMP_HIDDEN = 2 * HEAD_DIM
SEL_BLOCK = 64
N_SELECT = 16
WINDOW = 512
DIFF_HEADS = 8
DIFF_VDIM = 2 * HEAD_DIM
ROPE_THETA = 10000.0
NORM_EPS = 1e-6
FORCE_SCORE = 1e4
NEG_INF = -1e30
SCALE = HEAD_DIM ** -0.5
PAGE_SIZE = 128

LANES = 128
NSA_Q_W = NSA_HEADS * HEAD_DIM
NSA_KV_W = NSA_GROUPS * HEAD_DIM
DIFF_W = DIFF_HEADS * DIFF_VDIM
PAD_ROWS = 128

COL_Q = 0
COL_Z_NSA = COL_Q + NSA_Q_W
COL_DQ = COL_Z_NSA + NSA_Q_W
COL_DK = COL_DQ + DIFF_W
COL_DV = COL_DK + DIFF_W
COL_Z_DIFF = COL_DV + DIFF_W
COL_KV = COL_Z_DIFF + DIFF_W
COL_GA = COL_KV + 6 * NSA_KV_W
PROJ_TN = 512
VMEM_LIMIT = 56 * 1024 * 1024


def _sigmoid(x):
    return 1.0 / (1.0 + jnp.exp(-x))


def _dot(a, b):
    return jnp.dot(a, b, preferred_element_type=F32)


def _dot_nt(a, b):
    return lax.dot_general(a, b, (((1,), (1,)), ((), ())), preferred_element_type=F32)


def _params(*sem):
    return pltpu.CompilerParams(dimension_semantics=sem, vmem_limit_bytes=VMEM_LIMIT)


def _rmsnorm_kernel(x_ref, g_ref, o_ref):
    x = x_ref[...]
    ms = jnp.mean(x * x, axis=-1, keepdims=True)
    o_ref[...] = (x * lax.rsqrt(ms + NORM_EPS) * g_ref[...]).astype(o_ref.dtype)


def _rmsnorm(x2d, g, out_dtype):
    m, d = x2d.shape
    tm = min(256, m)
    return pl.pallas_call(
        _rmsnorm_kernel,
        grid=(m // tm,),
        in_specs=[pl.BlockSpec((tm, d), lambda i: (i, 0)), pl.BlockSpec((1, d), lambda i: (0, 0))],
        out_specs=pl.BlockSpec((tm, d), lambda i: (i, 0)),
        out_shape=jax.ShapeDtypeStruct((m, d), out_dtype),
        compiler_params=_params("parallel"),
        name="rmsnorm",
    )(x2d, g.reshape(1, d))


def _proj_layout(d_model):
    col_gb = COL_GA + d_model
    col_gates = col_gb + d_model
    n_cols = col_gates + PROJ_TN
    return col_gb, col_gates, n_cols


def _pack_w_in(w, d_model):
    splits = (NSA_Q_W, 6 * NSA_KV_W, 3 * NSA_HEADS, NSA_Q_W, DIFF_W, DIFF_W, DIFF_W, DIFF_W, d_model, d_model)
    cuts = [sum(splits[:i + 1]) for i in range(len(splits) - 1)]
    q, kv, gates, z_nsa, dq, dk, dv, z_diff, g_a, g_b = jnp.split(w.T, cuts, axis=0)
    gates = gates.reshape(NSA_GROUPS, NSA_REP * 3, w.shape[0])
    gates = jnp.pad(gates, ((0, 0), (0, LANES - NSA_REP * 3), (0, 0))).reshape(NSA_GROUPS * LANES, w.shape[0])
    return jnp.concatenate([q, z_nsa, dq, dk, dv, z_diff, kv, g_a, g_b, gates], axis=0).astype(BF16)


def _proj_kernel(h_ref, w_ref, cos_ref, sin_ref, o_ref, *, rope_blocks, silu_blocks, sig_start, row_chunk):
    j = pl.program_id(1)

    def _in(blocks):
        pred = None
        for lo, hi in blocks:
            c = (j >= lo) & (j < hi)
            pred = c if pred is None else (pred | c)
        return pred

    is_rope = _in(rope_blocks)
    is_silu = _in(silu_blocks)
    is_sig = j >= sig_start
    w = w_ref[...]
    for r0 in range(0, o_ref.shape[0], row_chunk):
        rows = slice(r0, r0 + row_chunk)
        x = _dot_nt(h_ref[rows, :], w)
        a = jnp.where(is_rope, cos_ref[rows, :], 1.0)
        b = jnp.where(is_rope, sin_ref[rows, :], 0.0)
        lin = jnp.concatenate(
            [x[:, c * HEAD_DIM:(c + 1) * HEAD_DIM] * a
             + pltpu.roll(x[:, c * HEAD_DIM:(c + 1) * HEAD_DIM], HEAD_DIM // 2, 1) * b
             for c in range(x.shape[1] // HEAD_DIM)], axis=1)
        sg = _sigmoid(x)
        o_ref[rows, :] = jnp.where(is_sig, sg, jnp.where(is_silu, x * sg, lin))


def _project(h, w_packed, cos, sin):
    m, d = h.shape
    n_cols = w_packed.shape[0]
    tm = min(1024, m)
    tn = PROJ_TN
    b = lambda col: col // tn
    kv0 = b(COL_KV)
    rope_blocks = ((b(COL_Q), b(COL_Z_NSA)), (b(COL_DQ), b(COL_DV)),
                   (kv0, kv0 + 1), (kv0 + 2, kv0 + 3), (kv0 + 4, kv0 + 5))
    silu_blocks = ((b(COL_Z_NSA), b(COL_DQ)), (b(COL_Z_DIFF), b(COL_KV)))
    kern = functools.partial(_proj_kernel, rope_blocks=rope_blocks, silu_blocks=silu_blocks, sig_start=b(COL_GA),
                             row_chunk=min(256, tm))
    return pl.pallas_call(
        kern,
        grid=(m // tm, n_cols // tn),
        in_specs=[pl.BlockSpec((tm, d), lambda i, j: (i, 0)),
                  pl.BlockSpec((tn, d), lambda i, j: (j, 0)),
                  pl.BlockSpec((tm, HEAD_DIM), lambda i, j: (i, 0)),
                  pl.BlockSpec((tm, HEAD_DIM), lambda i, j: (i, 0))],
        out_specs=pl.BlockSpec((tm, tn), lambda i, j: (i, j)),
        out_shape=jax.ShapeDtypeStruct((m, n_cols), F32),
        compiler_params=_params("parallel", "arbitrary"),
        name="in_proj",
    )(h, w_packed, cos, sin)


def _rope_tables(pos):
    half = HEAD_DIM // 2
    inv = ROPE_THETA ** (-jnp.arange(half, dtype=F32) / half)
    ang = pos.astype(F32)[:, None] * inv[None, :]
    cos, sin = jnp.cos(ang), jnp.sin(ang)
    return jnp.concatenate([cos, cos], axis=1), jnp.concatenate([-sin, sin], axis=1)


def _pos_term_kernel(pe_ref, w1_ref, o_ref):
    pe = jnp.broadcast_to(pe_ref[0], (8, pe_ref.shape[2])).astype(BF16)
    o_ref[0] = _dot(pe, w1_ref[0])


def _pos_terms(pe_kv, w1_kv):
    kdim = pe_kv.shape[1]
    out = pl.pallas_call(
        _pos_term_kernel,
        grid=(2,),
        in_specs=[pl.BlockSpec((1, 1, kdim), lambda i: (i, 0, 0)),
                  pl.BlockSpec((1, kdim, CMP_HIDDEN), lambda i: (i, 0, 0))],
        out_specs=pl.BlockSpec((1, 8, CMP_HIDDEN), lambda i: (i, 0, 0)),
        out_shape=jax.ShapeDtypeStruct((2, 8, CMP_HIDDEN), F32),
        compiler_params=_params("arbitrary"),
        name="cmp_pos_term",
    )(pe_kv.reshape(2, 1, kdim), w1_kv)
    return out[:, 0:1, :]


def _finish_compress(first, second, pos, w2):
    rows = first.shape[0]
    shifted = pltpu.roll(second, rows - 1, 0)
    pre = first + shifted + pos
    hid = pre * _sigmoid(pre)
    return _dot(hid.astype(BF16), w2)


def _compress_prompt_kernel(x0_ref, x1_ref, x2_ref, x3_ref, w1_ref, pos_ref, w2_ref, o_ref, *, nh):
    x_refs = (x0_ref, x1_ref, x2_ref, x3_ref)
    acc = jnp.zeros((NSA_GROUPS * nh, 2 * CMP_HIDDEN), F32)
    for s in range(CMP_STRIDE):
        xs = jnp.concatenate([x[pl.ds(s, nh, stride=CMP_STRIDE), :] for x in x_refs], axis=0)
        acc = acc + _dot(xs.astype(BF16), w1_ref[0, s])
    out = _finish_compress(acc[:, :CMP_HIDDEN], acc[:, CMP_HIDDEN:], pos_ref[0], w2_ref[0])
    o_ref[0, 0] = out.reshape(NSA_GROUPS, nh, HEAD_DIM)


def _compress_prompt(proj, w1ab_kv, pos_kv, w2_kv, n, seq):
    nh = seq // CMP_STRIDE
    cb = COL_KV // HEAD_DIM
    kern = functools.partial(_compress_prompt_kernel, nh=nh)

    def group_spec(g):
        return pl.BlockSpec((seq, HEAD_DIM), lambda c, i: (i, cb + c * NSA_GROUPS + g))

    return pl.pallas_call(
        kern,
        grid=(2, n),
        in_specs=[group_spec(g) for g in range(NSA_GROUPS)]
        + [pl.BlockSpec((1, CMP_STRIDE, HEAD_DIM, 2 * CMP_HIDDEN), lambda c, i: (c, 0, 0, 0)),
           pl.BlockSpec((1, 1, CMP_HIDDEN), lambda c, i: (c, 0, 0)),
           pl.BlockSpec((1, CMP_HIDDEN, HEAD_DIM), lambda c, i: (c, 0, 0))],
        out_specs=pl.BlockSpec((1, 1, NSA_GROUPS, nh, HEAD_DIM), lambda c, i: (c, i, 0, 0, 0)),
        out_shape=jax.ShapeDtypeStruct((2, n, NSA_GROUPS, nh, HEAD_DIM), F32),
        compiler_params=_params("arbitrary", "arbitrary"),
        name="compress_prompt",
    )(proj, proj, proj, proj, w1ab_kv, pos_kv, w2_kv)


QK_SCALE = SCALE * math.log2(math.e)


def _scaled_q(q):
    return (q * QK_SCALE).astype(BF16)


def _cmp_probs(qs, kc, trow, nc):
    s = _dot_nt(qs, kc)
    cidx = lax.broadcasted_iota(jnp.int32, s.shape, 1)
    vmask = ((cidx * CMP_STRIDE + CMP_BLOCK) <= trow + 1) & (cidx < nc)
    sm = jnp.where(vmask, s, NEG_INF)
    e = jnp.exp2(sm - jnp.max(sm, axis=-1, keepdims=True))
    p = e / jnp.sum(e, axis=-1, keepdims=True)
    return jnp.where(vmask, p, 0.0)


def _block_scores(psum, nblk):
    nc_rows = psum.shape[1]
    cc = lax.broadcasted_iota(jnp.int32, (nc_rows, LANES), 0) * CMP_STRIDE
    jj = lax.broadcasted_iota(jnp.int32, (nc_rows, LANES), 1)
    cover = (cc < jj * SEL_BLOCK + SEL_BLOCK) & (cc + CMP_BLOCK > jj * SEL_BLOCK) & (jj < nblk)
    cover = jnp.where(cover, 1.0, 0.0).astype(BF16)
    hi = psum.astype(BF16)
    r1 = psum - hi.astype(F32)
    mid = r1.astype(BF16)
    lo = (r1 - mid.astype(F32)).astype(BF16)
    return _dot(hi, cover) + _dot(mid, cover) + _dot(lo, cover)


def _rank_select(ranked, nblk, nsel):
    rt = ranked.T[0:nblk]
    ji = lax.broadcasted_iota(jnp.int32, rt.shape, 0)
    rank = jnp.zeros(rt.shape, F32)
    for i in range(nblk):
        row = rt[i:i + 1, :]
        beats = (row > rt) | ((row == rt) & (ji > i))
        rank = rank + jnp.where(beats, 1.0, 0.0)
    sel_t = jnp.where(rank < nsel, 1.0, 0.0)
    sel_t = jnp.concatenate([sel_t, jnp.zeros((LANES - nblk, rt.shape[1]), F32)], axis=0)
    return sel_t.T


def _expand_blocks(sel01, n_keys):
    jj = lax.broadcasted_iota(jnp.int32, (LANES, n_keys), 0)
    kk = lax.broadcasted_iota(jnp.int32, (LANES, n_keys), 1)
    expand = jnp.where((kk // SEL_BLOCK) == jj, 1.0, 0.0).astype(BF16)
    return _dot(sel01, expand)


def _softmax_weights(s):
    e = jnp.exp2(s - jnp.max(s, axis=-1, keepdims=True))
    return e, jnp.sum(e, axis=-1, keepdims=True)


def _masked_attend(qs, k, v, mask):
    e, l = _softmax_weights(jnp.where(mask, _dot_nt(qs, k), NEG_INF))
    return _dot(e.astype(BF16), v) / l


def _biased_attend(qs, k, v, bias):
    e, l = _softmax_weights(_dot_nt(qs, k) + bias)
    return _dot(e.astype(BF16), v) / l


def _online_update(m, l, acc, s, v):
    m_new = jnp.maximum(m, jnp.max(s, axis=-1, keepdims=True))
    alpha = jnp.exp2(m - m_new)
    e = jnp.exp2(s - m_new)
    return m_new, alpha * l + jnp.sum(e, axis=-1, keepdims=True), alpha * acc + _dot(e.astype(BF16), v)


def _nsa_prompt_kernel(q_ref, kc_ref, vc_ref, ks_ref, vs_ref, kw_ref, vw_ref, gate_ref, z_ref, o_ref,
                       ksb, vsb, kwb, vwb, osel, *, tq, seq, nc, nblk, nsel, kstep):
    qb = pl.program_id(2)
    q0 = qb * tq

    @pl.when(qb == 0)
    def _():
        ksb[...] = ks_ref[...].astype(BF16)
        vsb[...] = vs_ref[...].astype(BF16)
        kwb[...] = kw_ref[...].astype(BF16)
        vwb[...] = vw_ref[...].astype(BF16)

    q = q_ref[...]
    qs = _scaled_q(jnp.concatenate([q[:, r * HEAD_DIM:(r + 1) * HEAD_DIM] for r in range(NSA_REP)], axis=0))

    kc = kc_ref[0, 0, 0].astype(BF16)
    vc = vc_ref[0, 0, 0].astype(BF16)
    ncp = kc.shape[0]
    trow = (lax.broadcasted_iota(jnp.int32, (NSA_REP * tq, ncp), 0) & (tq - 1)) + q0
    p = _cmp_probs(qs, kc, trow, nc)
    o_cmp = _dot(p.astype(BF16), vc)

    psum = p[0:tq] + p[tq:2 * tq] + p[2 * tq:3 * tq] + p[3 * tq:4 * tq]
    score = _block_scores(psum, nblk)
    jl = lax.broadcasted_iota(jnp.int32, (tq, LANES), 1)
    tl = lax.broadcasted_iota(jnp.int32, (tq, LANES), 0) + q0
    cur = tl // SEL_BLOCK
    forced = (jl == 0) | (jl == cur) | (jl == cur - 1)
    ranked = jnp.where(forced, FORCE_SCORE, jnp.where(jl <= cur, score, -1.0))
    sel01 = jnp.where(jl <= cur, _rank_select(ranked, nblk, nsel), 0.0).astype(BF16)

    for v in range(seq // kstep):
        ext = (v + 1) * kstep

        @pl.when(q0 // kstep == v)
        def _(ext=ext):
            key_sel = _expand_blocks(sel01, ext)
            kpos = lax.broadcasted_iota(jnp.int32, (tq, ext), 1)
            tpos = lax.broadcasted_iota(jnp.int32, (tq, ext), 0) + q0
            bias = jnp.where((key_sel > 0.5) & (kpos <= tpos), 0.0, NEG_INF)
            ks = ksb[0:ext, :]
            vs = vsb[0:ext, :]
            for r in range(NSA_REP):
                osel[r * tq:(r + 1) * tq, :] = _biased_attend(qs[r * tq:(r + 1) * tq], ks, vs, bias)

    span = min(tq + WINDOW, seq)
    start = pl.multiple_of(jnp.clip(q0 + tq - span, 0, seq - span), tq)
    wk = lax.broadcasted_iota(jnp.int32, (tq, span), 1) + start
    wd = (lax.broadcasted_iota(jnp.int32, (tq, span), 0) + q0) - wk
    win_bias = jnp.where((wd >= 0) & (wd < WINDOW), 0.0, NEG_INF)
    kw = kwb[pl.ds(start, span), :]
    vw = vwb[pl.ds(start, span), :]
    gt = gate_ref[...]
    z = z_ref[...]
    for r in range(NSA_REP):
        o_sel = osel[r * tq:(r + 1) * tq, :]
        o_win = _biased_attend(qs[r * tq:(r + 1) * tq], kw, vw, win_bias)
        o = (gt[:, 3 * r:3 * r + 1] * o_cmp[r * tq:(r + 1) * tq] + gt[:, 3 * r + 1:3 * r + 2] * o_sel
             + gt[:, 3 * r + 2:3 * r + 3] * o_win)
        sl = slice(r * HEAD_DIM, (r + 1) * HEAD_DIM)
        o_ref[:, sl] = (o * z[:, sl]).astype(o_ref.dtype)


def _nsa_prompt(proj, kvc, col_gates, n, seq):
    tq = 128
    nqb = seq // tq
    nh = kvc.shape[3]
    nblk = seq // SEL_BLOCK
    kern = functools.partial(_nsa_prompt_kernel, tq=tq, seq=seq, nc=nh - 1, nblk=nblk, nsel=min(N_SELECT, nblk),
                             kstep=512)
    gw = NSA_REP * HEAD_DIM
    kvb = COL_KV // HEAD_DIM

    def kv_spec(which):
        return pl.BlockSpec((seq, HEAD_DIM), lambda i, g, qb: (i, kvb + which * NSA_GROUPS + g))

    return pl.pallas_call(
        kern,
        grid=(n, NSA_GROUPS, nqb),
        in_specs=[pl.BlockSpec((tq, gw), lambda i, g, qb: (i * nqb + qb, COL_Q // gw + g)),
                  pl.BlockSpec((1, 1, 1, nh, HEAD_DIM), lambda i, g, qb: (0, i, g, 0, 0)),
                  pl.BlockSpec((1, 1, 1, nh, HEAD_DIM), lambda i, g, qb: (1, i, g, 0, 0)),
                  kv_spec(2), kv_spec(3), kv_spec(4), kv_spec(5),
                  pl.BlockSpec((tq, LANES), lambda i, g, qb: (i * nqb + qb, col_gates // LANES + g)),
                  pl.BlockSpec((tq, gw), lambda i, g, qb: (i * nqb + qb, COL_Z_NSA // gw + g))],
        out_specs=pl.BlockSpec((tq, gw), lambda i, g, qb: (i * nqb + qb, g)),
        out_shape=jax.ShapeDtypeStruct((n * seq, NSA_Q_W), BF16),
        scratch_shapes=[pltpu.VMEM((seq, HEAD_DIM), BF16)] * 4 + [pltpu.VMEM((NSA_REP * tq, HEAD_DIM), F32)],
        compiler_params=_params("arbitrary", "arbitrary", "arbitrary"),
        name="nsa_prompt",
    )(proj, kvc, kvc, proj, proj, proj, proj, proj, proj)


def _lambda(lq1_ref, lk1_ref, lq2_ref, lk2_ref, lam_init):
    a = jnp.sum(lq1_ref[...] * lk1_ref[...], axis=-1, keepdims=True)
    b = jnp.sum(lq2_ref[...] * lk2_ref[...], axis=-1, keepdims=True)
    return jnp.exp(a) - jnp.exp(b) + lam_init


def _diff_finish(o, g, z, lam_init):
    ms = jnp.mean(o * o, axis=-1, keepdims=True)
    return o * lax.rsqrt(ms + NORM_EPS) * g * (1.0 - lam_init) * z


def _diff_prompt_kernel(lq1_ref, lk1_ref, lq2_ref, lk2_ref, q_ref, k_ref, v_ref, z_ref, g_ref, o_ref,
                        kb, vb, *, tq, seq, lam_init, kstep):
    qb = pl.program_id(2)
    q0 = qb * tq

    @pl.when(qb == 0)
    def _():
        kb[...] = k_ref[...].astype(BF16)
        vb[...] = v_ref[...].astype(BF16)

    lam = _lambda(lq1_ref, lk1_ref, lq2_ref, lk2_ref, lam_init)
    q = _scaled_q(q_ref[...])

    for v in range(seq // kstep):
        ext = (v + 1) * kstep

        @pl.when(q0 // kstep == v)
        def _(ext=ext):
            causal = (lax.broadcasted_iota(jnp.int32, (tq, ext), 1)
                      <= lax.broadcasted_iota(jnp.int32, (tq, ext), 0) + q0)
            bias = jnp.where(causal, 0.0, NEG_INF)
            k = kb[0:ext, :]
            maps = []
            for c in range(2):
                sl = slice(c * HEAD_DIM, (c + 1) * HEAD_DIM)
                e, l = _softmax_weights(_dot_nt(q[:, sl], k[:, sl]) + bias)
                maps.append(e / l)
            w = maps[0] - lam * maps[1]
            o = _dot(w.astype(BF16), vb[0:ext, :])
            o_ref[...] = _diff_finish(o, g_ref[...], z_ref[...], lam_init).astype(o_ref.dtype)


def _diff_prompt(proj, lams, diff_g, lam_init, n, seq):
    tq = 256
    nqb = seq // tq
    kern = functools.partial(_diff_prompt_kernel, tq=tq, seq=seq, lam_init=lam_init, kstep=512)
    hw = DIFF_VDIM
    lam_spec = pl.BlockSpec((1, HEAD_DIM), lambda i, h, qb: (0, 0))
    return pl.pallas_call(
        kern,
        grid=(n, DIFF_HEADS, nqb),
        in_specs=[lam_spec, lam_spec, lam_spec, lam_spec,
                  pl.BlockSpec((tq, hw), lambda i, h, qb: (i * nqb + qb, COL_DQ // hw + h)),
                  pl.BlockSpec((seq, hw), lambda i, h, qb: (i, COL_DK // hw + h)),
                  pl.BlockSpec((seq, hw), lambda i, h, qb: (i, COL_DV // hw + h)),
                  pl.BlockSpec((tq, hw), lambda i, h, qb: (i * nqb + qb, COL_Z_DIFF // hw + h)),
                  pl.BlockSpec((1, hw), lambda i, h, qb: (0, 0))],
        out_specs=pl.BlockSpec((tq, hw), lambda i, h, qb: (i * nqb + qb, h)),
        out_shape=jax.ShapeDtypeStruct((n * seq, DIFF_W), BF16),
        scratch_shapes=[pltpu.VMEM((seq, hw), BF16)] * 2,
        compiler_params=_params("arbitrary", "arbitrary", "arbitrary"),
        name="diff_prompt",
    )(*lams, proj, proj, proj, proj, diff_g)


def _mix1_kernel(a_ref, wa_ref, b_ref, wb_ref, ga_ref, gb_ref, o_ref):
    pa = _dot(a_ref[...].astype(BF16), wa_ref[...])
    pb = _dot(b_ref[...].astype(BF16), wb_ref[...])
    o_ref[...] = (ga_ref[...] * pa + gb_ref[...] * pb).astype(o_ref.dtype)


def _mix1(o_nsa, w_nsa, o_diff, w_diff, proj, col_gb):
    m = o_nsa.shape[0]
    d_model = w_nsa.shape[1]
    tm = min(1024, m)
    tn = 512
    return pl.pallas_call(
        _mix1_kernel,
        grid=(m // tm, d_model // tn),
        in_specs=[pl.BlockSpec((tm, NSA_Q_W), lambda i, j: (i, 0)),
                  pl.BlockSpec((NSA_Q_W, tn), lambda i, j: (0, j)),
                  pl.BlockSpec((tm, DIFF_W), lambda i, j: (i, 0)),
                  pl.BlockSpec((DIFF_W, tn), lambda i, j: (0, j)),
                  pl.BlockSpec((tm, tn), lambda i, j: (i, COL_GA // tn + j)),
                  pl.BlockSpec((tm, tn), lambda i, j: (i, col_gb // tn + j))],
        out_specs=pl.BlockSpec((tm, tn), lambda i, j: (i, j)),
        out_shape=jax.ShapeDtypeStruct((m, d_model), BF16),
        compiler_params=_params("parallel", "arbitrary"),
        name="mix_gated_proj",
    )(o_nsa, w_nsa, o_diff, w_diff, proj, proj)


def _mix2_kernel(m_ref, w_ref, x_ref, g_ref, o_ref, *, n_chunk):
    k = pl.program_id(1)
    d_model = o_ref.shape[1]
    cols = [slice(n0, n0 + n_chunk) for n0 in range(0, d_model, n_chunk)]

    @pl.when(k == 0)
    def _():
        o_ref[...] = x_ref[...]

    mm = m_ref[...]
    for sl in cols:
        o_ref[:, sl] += _dot(mm, w_ref[:, sl])

    @pl.when(k == pl.num_programs(1) - 1)
    def _():
        ss = jnp.zeros((o_ref.shape[0], 1), F32)
        for sl in cols:
            y = o_ref[:, sl]
            ss = ss + jnp.sum(y * y, axis=-1, keepdims=True)
        r = lax.rsqrt(ss * (1.0 / d_model) + NORM_EPS)
        for sl in cols:
            o_ref[:, sl] = o_ref[:, sl] * r * g_ref[:, sl]


def _mix2(merged, w_out, x2d, final_g):
    m, d_model = x2d.shape
    tm = min(512, m)
    tk = 512
    return pl.pallas_call(
        functools.partial(_mix2_kernel, n_chunk=1024),
        grid=(m // tm, d_model // tk),
        in_specs=[pl.BlockSpec((tm, tk), lambda i, k: (i, k)),
                  pl.BlockSpec((tk, d_model), lambda i, k: (k, 0)),
                  pl.BlockSpec((tm, d_model), lambda i, k: (i, 0)),
                  pl.BlockSpec((1, d_model), lambda i, k: (0, 0))],
        out_specs=pl.BlockSpec((tm, d_model), lambda i, k: (i, 0)),
        out_shape=jax.ShapeDtypeStruct((m, d_model), F32),
        compiler_params=_params("parallel", "arbitrary"),
        name="out_proj_residual_norm",
    )(merged, w_out, x2d, final_g.reshape(1, d_model))


def _cmp_partial_kernel(pt_ref, *refs, nbs):
    del pt_ref
    x_refs, w_ref, o_ref = refs[:nbs], refs[nbs], refs[nbs + 1]
    hpp = PAGE_SIZE // CMP_STRIDE
    acc = jnp.zeros((nbs * NSA_GROUPS * hpp, 2 * CMP_HIDDEN), F32)

    def token_rows(s):
        return jnp.concatenate([x[pl.ds(NSA_GROUPS * s + g, hpp, stride=SLAB_PITCH), :]
                                for x in flat for g in range(NSA_GROUPS)], axis=0)

    flat = [x.reshape(hpp * SLAB_PITCH, HEAD_DIM) for x in x_refs]

    for s in range(0, CMP_STRIDE, 2):
        xs = jnp.concatenate([token_rows(s), token_rows(s + 1)], axis=1).astype(BF16)
        acc = acc + _dot(xs, w_ref[s // 2])
    o_ref[...] = acc.reshape(nbs, NSA_GROUPS, hpp, 2 * CMP_HIDDEN)


SLAB_ROWS = CMP_STRIDE * NSA_GROUPS
SLAB_PITCH = SLAB_ROWS + 8


def _cmp_partials(cache, page_flat, w1ab, n, n_pages):
    pool = cache.shape[0]
    hpp = PAGE_SIZE // CMP_STRIDE
    view = cache.reshape(pool, hpp, SLAB_ROWS, HEAD_DIM)
    w_pairs = w1ab.reshape(CMP_STRIDE // 2, 2 * HEAD_DIM, 2 * CMP_HIDDEN)
    nbs = min(16, n)
    kern = functools.partial(_cmp_partial_kernel, nbs=nbs)

    def page_spec(i):
        return pl.BlockSpec((1, hpp, SLAB_PITCH, HEAD_DIM),
                            lambda nb, p, pt: (pt[(nb * nbs + i) * n_pages + p], 0, 0, 0))

    return pl.pallas_call(
        kern,
        grid_spec=pltpu.PrefetchScalarGridSpec(
            num_scalar_prefetch=1,
            grid=(n // nbs, n_pages),
            in_specs=[page_spec(i) for i in range(nbs)]
            + [pl.BlockSpec((CMP_STRIDE // 2, 2 * HEAD_DIM, 2 * CMP_HIDDEN), lambda nb, p, pt: (0, 0, 0))],
            out_specs=pl.BlockSpec((nbs, NSA_GROUPS, hpp, 2 * CMP_HIDDEN), lambda nb, p, pt: (nb, 0, p, 0)),
        ),
        out_shape=jax.ShapeDtypeStruct((n, NSA_GROUPS, n_pages * hpp, 2 * CMP_HIDDEN), F32),
        compiler_params=_params("parallel", "arbitrary"),
        name="cmp_partials_sample",
    )(page_flat, *([view] * nbs), w_pairs)


def _cmp_sample_kernel(fsk_ref, fsv_ref, pos_ref, w2_ref, q_ref, ocmp_ref, sel_ref,
                       *, nb2, nh, nc, past_len, dec, nblk, nsel):
    rows = nb2 * NSA_GROUPS * nh

    def compress(fs_ref, c):
        fs = fs_ref[...].reshape(rows, 2 * CMP_HIDDEN)
        return _finish_compress(fs[:, :CMP_HIDDEN], fs[:, CMP_HIDDEN:], pos_ref[c], w2_ref[c])

    kc = compress(fsk_ref, 0).astype(BF16)
    vc = compress(fsv_ref, 1).astype(BF16)
    trow = (lax.broadcasted_iota(jnp.int32, (NSA_REP * dec, nh), 0) & (dec - 1)) + past_len
    scores = []
    for i in range(nb2):
        qi = q_ref[i * dec:(i + 1) * dec, :]
        for g in range(NSA_GROUPS):
            base = (i * NSA_GROUPS + g) * nh
            c0 = g * NSA_REP * HEAD_DIM
            qg = _scaled_q(jnp.concatenate([qi[:, c0 + r * HEAD_DIM:c0 + (r + 1) * HEAD_DIM] for r in range(NSA_REP)],
                                           axis=0))
            p = _cmp_probs(qg, kc[base:base + nh], trow, nc)
            o = _dot(p.astype(BF16), vc[base:base + nh])
            for r in range(NSA_REP):
                ocmp_ref[i * dec:(i + 1) * dec, c0 + r * HEAD_DIM:c0 + (r + 1) * HEAD_DIM] = o[r * dec:(r + 1) * dec]
            psum = p[0:dec] + p[dec:2 * dec] + p[2 * dec:3 * dec] + p[3 * dec:4 * dec]
            scores.append(_block_scores(psum, nblk))
    score = jnp.concatenate(scores, axis=0)
    jl = lax.broadcasted_iota(jnp.int32, score.shape, 1)
    ranked = jnp.where((jl == 0) | (jl == nblk - 1), FORCE_SCORE, score)
    sel_ref[...] = _rank_select(ranked, nblk, nsel).reshape(nb2, NSA_GROUPS, dec, LANES)


def _cmp_sample(fs_k, fs_v, pos_kv, w2_kv, proj_s, n, dec, past_len):
    nh = fs_k.shape[2]
    nb2 = min(4, n)
    nblk = past_len // SEL_BLOCK
    kern = functools.partial(_cmp_sample_kernel, nb2=nb2, nh=nh, nc=nh - 1, past_len=past_len, dec=dec,
                             nblk=nblk, nsel=min(N_SELECT - 1, nblk))
    fs_spec = pl.BlockSpec((nb2, NSA_GROUPS, nh, 2 * CMP_HIDDEN), lambda i: (i, 0, 0, 0))
    return pl.pallas_call(
        kern,
        grid=(n // nb2,),
        in_specs=[fs_spec, fs_spec,
                  pl.BlockSpec((2, 1, CMP_HIDDEN), lambda i: (0, 0, 0)),
                  pl.BlockSpec((2, CMP_HIDDEN, HEAD_DIM), lambda i: (0, 0, 0)),
                  pl.BlockSpec((nb2 * dec, NSA_Q_W), lambda i: (i, 0))],
        out_specs=[pl.BlockSpec((nb2 * dec, NSA_Q_W), lambda i: (i, 0)),
                   pl.BlockSpec((nb2, NSA_GROUPS, dec, LANES), lambda i: (i, 0, 0, 0))],
        out_shape=[jax.ShapeDtypeStruct((n * dec, NSA_Q_W), F32),
                   jax.ShapeDtypeStruct((n, NSA_GROUPS, dec, LANES), F32)],
        compiler_params=_params("parallel"),
        name="cmp_attend_select_sample",
    )(fs_k, fs_v, pos_kv, w2_kv, proj_s)


def _pad_new(x):
    return jnp.concatenate([x, jnp.zeros((PAD_ROWS - x.shape[0], x.shape[1]), x.dtype)], axis=0).astype(BF16)


def _nsa_sample_kernel(pt_ref, *refs, n_pages, past_len, dec, wb):
    del pt_ref
    ksel_pages = refs[:n_pages]
    vsel_pages = refs[n_pages:2 * n_pages]
    (bufk_ref, bufv_ref, q_ref, ksn_ref, vsn_ref, kwn_ref, vwn_ref, ocmp_ref, gate_ref, z_ref, sel_ref,
     o_ref, nbk_ref, nbv_ref) = refs[2 * n_pages:]
    rows = NSA_REP * dec
    t_of_row = lambda shape: lax.broadcasted_iota(jnp.int32, shape, 0) & (dec - 1)

    def group_rows(ref, g, n_tok):
        return ref[0, pl.ds(g, n_tok, stride=NSA_GROUPS), :]

    keep_rows = (wb - dec) * NSA_GROUPS
    for buf_ref, new_ref, out_ref in ((bufk_ref, kwn_ref, nbk_ref), (bufv_ref, vwn_ref, nbv_ref)):
        out_ref[0, 0:keep_rows, :] = buf_ref[0, dec * NSA_GROUPS:wb * NSA_GROUPS, :]
        for g in range(NSA_GROUPS):
            out_ref[0, pl.ds(keep_rows + g, dec, stride=NSA_GROUPS), :] = new_ref[:, g * HEAD_DIM:(g + 1) * HEAD_DIM]

    jn = lax.broadcasted_iota(jnp.int32, (rows, PAD_ROWS), 1)
    new_ok = jnp.where((jn < dec) & (jn <= t_of_row((rows, PAD_ROWS))), 1.0, 0.0)
    n_win = wb + PAD_ROWS
    wi = lax.broadcasted_iota(jnp.int32, (rows, n_win), 1)
    wkpos = jnp.where(wi < wb, past_len - wb + wi, past_len + wi - wb)
    wd = (past_len + t_of_row((rows, n_win))) - wkpos
    win_mask = (wd >= 0) & (wd < WINDOW) & (wi < wb + dec)

    q = q_ref[...]
    gt = gate_ref[...]
    z = z_ref[...]
    for g in range(NSA_GROUPS):
        sl = slice(g * HEAD_DIM, (g + 1) * HEAD_DIM)
        c0 = g * NSA_REP * HEAD_DIM
        qg = _scaled_q(jnp.concatenate([q[:, c0 + r * HEAD_DIM:c0 + (r + 1) * HEAD_DIM] for r in range(NSA_REP)],
                                       axis=0))
        kg = jnp.concatenate([group_rows(pg, g, PAGE_SIZE).astype(BF16) for pg in ksel_pages]
                             + [_pad_new(ksn_ref[:, sl])], axis=0)
        vg = jnp.concatenate([group_rows(pg, g, PAGE_SIZE).astype(BF16) for pg in vsel_pages]
                             + [_pad_new(vsn_ref[:, sl])], axis=0)
        key_sel = _expand_blocks(sel_ref[0, g].astype(BF16), past_len)
        ok = jnp.concatenate([jnp.concatenate([key_sel] * NSA_REP, axis=0), new_ok], axis=1)
        o_sel = _masked_attend(qg, kg, vg, ok > 0.5)
        kw = jnp.concatenate([group_rows(bufk_ref, g, wb).astype(BF16), _pad_new(kwn_ref[:, sl])], axis=0)
        vw = jnp.concatenate([group_rows(bufv_ref, g, wb).astype(BF16), _pad_new(vwn_ref[:, sl])], axis=0)
        o_win = _masked_attend(qg, kw, vw, win_mask)
        for r in range(NSA_REP):
            cs = slice(c0 + r * HEAD_DIM, c0 + (r + 1) * HEAD_DIM)
            rs = slice(r * dec, (r + 1) * dec)
            gi = g * LANES + 3 * r
            o = (gt[:, gi:gi + 1] * ocmp_ref[:, cs] + gt[:, gi + 1:gi + 2] * o_sel[rs]
                 + gt[:, gi + 2:gi + 3] * o_win[rs])
            o_ref[:, cs] = o * z[:, cs]


def _nsa_sample(cache_k, cache_v, buf_k, buf_v, page_flat, proj_s, o_cmp, sel, col_gates, n, dec, n_pages):
    pool = cache_k.shape[0]
    wb = buf_k.shape[1]
    past_len = n_pages * PAGE_SIZE
    kview = cache_k.reshape(pool, PAGE_SIZE * NSA_GROUPS, HEAD_DIM)
    vview = cache_v.reshape(pool, PAGE_SIZE * NSA_GROUPS, HEAD_DIM)
    bk = buf_k.reshape(n, wb * NSA_GROUPS, HEAD_DIM)
    bv = buf_v.reshape(n, wb * NSA_GROUPS, HEAD_DIM)
    kern = functools.partial(_nsa_sample_kernel, n_pages=n_pages, past_len=past_len, dec=dec, wb=wb)
    kvb = COL_KV // NSA_KV_W

    def page_spec(p):
        return pl.BlockSpec((1, PAGE_SIZE * NSA_GROUPS, HEAD_DIM), lambda i, pt: (pt[i * n_pages + p], 0, 0))

    def row_spec(width, col_block):
        return pl.BlockSpec((dec, width), lambda i, pt: (i, col_block))

    buf_spec = pl.BlockSpec((1, wb * NSA_GROUPS, HEAD_DIM), lambda i, pt: (i, 0, 0))
    in_specs = ([page_spec(p) for p in range(n_pages)] + [page_spec(p) for p in range(n_pages)]
                + [buf_spec, buf_spec,
                   row_spec(NSA_Q_W, COL_Q // NSA_Q_W),
                   row_spec(NSA_KV_W, kvb + 2), row_spec(NSA_KV_W, kvb + 3),
                   row_spec(NSA_KV_W, kvb + 4), row_spec(NSA_KV_W, kvb + 5),
                   row_spec(NSA_Q_W, 0),
                   row_spec(NSA_GROUPS * LANES, col_gates // (NSA_GROUPS * LANES)),
                   row_spec(NSA_Q_W, COL_Z_NSA // NSA_Q_W),
                   pl.BlockSpec((1, NSA_GROUPS, dec, LANES), lambda i, pt: (i, 0, 0, 0))])
    return pl.pallas_call(
        kern,
        grid_spec=pltpu.PrefetchScalarGridSpec(
            num_scalar_prefetch=1,
            grid=(n,),
            in_specs=in_specs,
            out_specs=[row_spec(NSA_Q_W, 0), buf_spec, buf_spec],
        ),
        out_shape=[jax.ShapeDtypeStruct((n * dec, NSA_Q_W), F32),
                   jax.ShapeDtypeStruct((n, wb * NSA_GROUPS, HEAD_DIM), F32),
                   jax.ShapeDtypeStruct((n, wb * NSA_GROUPS, HEAD_DIM), F32)],
        compiler_params=_params("parallel"),
        name="nsa_sample",
    )(page_flat, *([kview] * n_pages), *([vview] * n_pages), bk, bv,
      proj_s, proj_s, proj_s, proj_s, proj_s, o_cmp, proj_s, proj_s, sel)


def _diff_sample_kernel(pt_ref, *refs, ppc, dec, lam_init):
    del pt_ref
    k_pages = refs[:ppc]
    v_pages = refs[ppc:2 * ppc]
    (lq1_ref, lk1_ref, lq2_ref, lk2_ref, q_ref, kn_ref, vn_ref, z_ref, g_ref, o_ref,
     m_sc, l_sc, acc_sc) = refs[2 * ppc:]
    j = pl.program_id(1)
    last = pl.num_programs(1) - 1

    @pl.when(j == 0)
    def _():
        m_sc[...] = jnp.full(m_sc.shape, NEG_INF, F32)
        l_sc[...] = jnp.zeros(l_sc.shape, F32)
        acc_sc[...] = jnp.zeros(acc_sc.shape, F32)

    hh = 2 * DIFF_HEADS
    nrow = hh * dec
    page_rows = PAGE_SIZE * hh

    def rows_chd(x, swap=False):
        return jnp.concatenate(
            [x[:, h * DIFF_VDIM + (c ^ swap) * HEAD_DIM:h * DIFF_VDIM + ((c ^ swap) + 1) * HEAD_DIM]
             for c in range(2) for h in range(DIFF_HEADS)], axis=0)

    qall = _scaled_q(rows_chd(q_ref[...]))
    ri = lax.broadcasted_iota(jnp.int32, (nrow, page_rows), 0)
    li = lax.broadcasted_iota(jnp.int32, (nrow, page_rows), 1)
    past_ok = (li & (hh - 1)) == (ri // dec)

    m = m_sc[:, 0:1]
    l = l_sc[:, 0:1]
    acc = acc_sc[...]
    for kp, vp in zip(k_pages, v_pages):
        xb = kp[0].reshape(page_rows, HEAD_DIM).astype(BF16)
        s = jnp.where(past_ok, _dot_nt(qall, xb), NEG_INF)
        v_same = vp[0].reshape(page_rows, HEAD_DIM)
        v_swap = jnp.concatenate([vp[0, :, DIFF_HEADS:hh, :], vp[0, :, 0:DIFF_HEADS, :]], axis=1)
        w2 = jnp.concatenate([v_same, v_swap.reshape(page_rows, HEAD_DIM)], axis=1).astype(BF16)
        m, l, acc = _online_update(m, l, acc, s, w2)

    rn = lax.broadcasted_iota(jnp.int32, (nrow, nrow), 0)
    ln = lax.broadcasted_iota(jnp.int32, (nrow, nrow), 1)
    gate = jnp.where(j == last, 0, -nrow)
    new_ok = ((ln // dec) == (rn // dec)) & ((ln & (dec - 1)) <= (rn & (dec - 1)) + gate)
    s = jnp.where(new_ok, _dot_nt(qall, rows_chd(kn_ref[...]).astype(BF16)), NEG_INF)
    vn = vn_ref[...]
    w2 = jnp.concatenate([rows_chd(vn), rows_chd(vn, swap=True)], axis=1).astype(BF16)
    m, l, acc = _online_update(m, l, acc, s, w2)
    m_sc[...] = jnp.broadcast_to(m, m_sc.shape)
    l_sc[...] = jnp.broadcast_to(l, l_sc.shape)
    acc_sc[...] = acc

    @pl.when(j == last)
    def _():
        lam = _lambda(lq1_ref, lk1_ref, lq2_ref, lk2_ref, lam_init)
        o = acc_sc[...] / l_sc[:, 0:1]
        half = DIFF_HEADS * dec
        for h in range(DIFF_HEADS):
            vsl = slice(h * DIFF_VDIM, (h + 1) * DIFF_VDIM)
            o1 = o[h * dec:(h + 1) * dec]
            r2 = o[half + h * dec:half + (h + 1) * dec]
            o2 = jnp.concatenate([r2[:, HEAD_DIM:], r2[:, :HEAD_DIM]], axis=1)
            o_ref[:, vsl] = _diff_finish(o1 - lam * o2, g_ref[...], z_ref[:, vsl], lam_init)


def _diff_cache_view(cache):
    pool = cache.shape[0]
    v = cache.reshape(pool, PAGE_SIZE, DIFF_HEADS, 2, HEAD_DIM).transpose(0, 1, 3, 2, 4)
    return v.reshape(pool, PAGE_SIZE, 2 * DIFF_HEADS, HEAD_DIM)


def _diff_sample(cache_k, cache_v, page_flat, proj_s, lams, diff_g, lam_init, n, dec, n_pages):
    kview = _diff_cache_view(cache_k)
    vview = _diff_cache_view(cache_v)
    n_chunks = 2 if n_pages % 2 == 0 else 1
    ppc = n_pages // n_chunks
    rows = 2 * DIFF_HEADS * dec
    kern = functools.partial(_diff_sample_kernel, ppc=ppc, dec=dec, lam_init=lam_init)

    def page_spec(p):
        return pl.BlockSpec((1, PAGE_SIZE, 2 * DIFF_HEADS, HEAD_DIM),
                            lambda i, j, pt: (pt[i * n_pages + j * ppc + p], 0, 0, 0))

    def row_spec(col0):
        return pl.BlockSpec((dec, DIFF_W), lambda i, j, pt: (i, col0 // DIFF_W))

    lam_spec = pl.BlockSpec((1, HEAD_DIM), lambda i, j, pt: (0, 0))
    in_specs = ([page_spec(p) for p in range(ppc)] + [page_spec(p) for p in range(ppc)]
                + [lam_spec] * 4
                + [row_spec(COL_DQ), row_spec(COL_DK), row_spec(COL_DV), row_spec(COL_Z_DIFF),
                   pl.BlockSpec((1, DIFF_VDIM), lambda i, j, pt: (0, 0))])
    return pl.pallas_call(
        kern,
        grid_spec=pltpu.PrefetchScalarGridSpec(
            num_scalar_prefetch=1,
            grid=(n, n_chunks),
            in_specs=in_specs,
            out_specs=pl.BlockSpec((dec, DIFF_W), lambda i, j, pt: (i, 0)),
            scratch_shapes=[pltpu.VMEM((rows, LANES), F32),
                            pltpu.VMEM((rows, LANES), F32),
                            pltpu.VMEM((rows, DIFF_VDIM), F32)],
        ),
        out_shape=jax.ShapeDtypeStruct((n * dec, DIFF_W), F32),
        compiler_params=_params("parallel", "arbitrary"),
        name="diff_sample",
    )(page_flat, *([kview] * ppc), *([vview] * ppc), *lams, proj_s, proj_s, proj_s, proj_s, diff_g)


def kernel(x_prompt, x_sample, cache_nsa_cmp_k, cache_nsa_cmp_v, cache_nsa_sel_k, cache_nsa_sel_v,
           state_nsa_win_k, state_nsa_win_v, cache_diff_k, cache_diff_v, page_table,
           norm_g, w_in, cmp_k_w1, cmp_k_pe, cmp_k_w2, cmp_v_w1, cmp_v_pe, cmp_v_w2,
           diff_lq1, diff_lk1, diff_lq2, diff_lk2, diff_norm_g, w_proj_nsa, w_proj_diff, w_out,
           final_norm_g):
    n_p, seq, d_model = x_prompt.shape
    n_s, dec, _ = x_sample.shape
    n_pages = page_table.shape[1]
    past_len = n_pages * PAGE_SIZE
    wb = state_nsa_win_k.shape[2]
    assert w_in.shape[0] == 1, "one layer only"
    assert seq % 256 == 0 and seq // CMP_STRIDE == LANES and seq >= WINDOW
    assert dec == 8 and wb == WINDOW and past_len % SEL_BLOCK == 0 and past_len // CMP_STRIDE == LANES
    assert ((past_len + dec) // CMP_STRIDE) * CMP_STRIDE <= past_len
    lam_init = 0.8 - 0.6 * math.exp(-0.3 * 0)
    col_gb, col_gates, _ = _proj_layout(d_model)

    w_packed = _pack_w_in(w_in[0], d_model)
    w1_kv = jnp.stack([cmp_k_w1[0], cmp_v_w1[0]]).astype(BF16)
    w1b = w1_kv.reshape(2, 2, CMP_STRIDE, HEAD_DIM, CMP_HIDDEN)
    w1ab_kv = jnp.concatenate([w1b[:, 0], w1b[:, 1]], axis=-1)
    pe_kv = jnp.stack([cmp_k_pe[0], cmp_v_pe[0]]).reshape(2, CMP_BLOCK * HEAD_DIM)
    w2_kv = jnp.stack([cmp_k_w2[0], cmp_v_w2[0]]).astype(BF16)
    lams = tuple(a[0].reshape(1, HEAD_DIM) for a in (diff_lq1, diff_lk1, diff_lq2, diff_lk2))
    diff_g = diff_norm_g[0].reshape(1, DIFF_VDIM)
    w_nsa = w_proj_nsa[0].astype(BF16)
    w_diff = w_proj_diff[0].astype(BF16)
    w_o = w_out[0].astype(BF16)
    page_flat = page_table.reshape(-1).astype(jnp.int32)
    pos_kv = _pos_terms(pe_kv, w1_kv)

    def split_rows(proj, n, s):
        kv0 = COL_KV
        seg = lambda i: proj[:, kv0 + i * NSA_KV_W:kv0 + (i + 1) * NSA_KV_W].reshape(1, n, s, NSA_GROUPS, HEAD_DIM)
        dk = proj[:, COL_DK:COL_DK + DIFF_W].reshape(1, n, s, DIFF_HEADS, DIFF_VDIM)
        dv = proj[:, COL_DV:COL_DV + DIFF_W].reshape(1, n, s, DIFF_HEADS, DIFF_VDIM)
        return [seg(i) for i in range(6)], dk, dv

    xp = x_prompt.reshape(n_p * seq, d_model)
    cos_p, sin_p = _rope_tables(jnp.arange(seq))
    proj_p = _project(_rmsnorm(xp, norm_g[0], BF16), w_packed,
                      jnp.tile(cos_p, (n_p, 1)), jnp.tile(sin_p, (n_p, 1)))
    kvc = _compress_prompt(proj_p, w1ab_kv, pos_kv, w2_kv, n_p, seq)
    o_nsa_p = _nsa_prompt(proj_p, kvc, col_gates, n_p, seq)
    o_diff_p = _diff_prompt(proj_p, lams, diff_g, lam_init, n_p, seq)
    merged_p = _mix1(o_nsa_p, w_nsa, o_diff_p, w_diff, proj_p, col_gb)
    y_prompt = _mix2(merged_p, w_o, xp, final_norm_g).reshape(n_p, seq, d_model)
    segs_p, dk_p, dv_p = split_rows(proj_p, n_p, seq)
    keep_p = min(WINDOW, seq)
    segs_p[4] = segs_p[4][:, :, seq - keep_p:]
    segs_p[5] = segs_p[5][:, :, seq - keep_p:]

    xs = x_sample.reshape(n_s * dec, d_model)
    cos_s, sin_s = _rope_tables(past_len + jnp.arange(dec))
    proj_s = _project(_rmsnorm(xs, norm_g[0], BF16), w_packed,
                      jnp.tile(cos_s, (n_s, 1)), jnp.tile(sin_s, (n_s, 1)))
    fs_k = _cmp_partials(cache_nsa_cmp_k[0], page_flat, w1ab_kv[0], n_s, n_pages)
    fs_v = _cmp_partials(cache_nsa_cmp_v[0], page_flat, w1ab_kv[1], n_s, n_pages)
    o_cmp_s, sel_s = _cmp_sample(fs_k, fs_v, pos_kv, w2_kv, proj_s, n_s, dec, past_len)
    o_nsa_s, win_k_s, win_v_s = _nsa_sample(cache_nsa_sel_k[0], cache_nsa_sel_v[0], state_nsa_win_k[0],
                                            state_nsa_win_v[0], page_flat, proj_s, o_cmp_s, sel_s, col_gates,
                                            n_s, dec, n_pages)
    o_diff_s = _diff_sample(cache_diff_k[0], cache_diff_v[0], page_flat, proj_s, lams, diff_g, lam_init,
                            n_s, dec, n_pages)
    merged_s = _mix1(o_nsa_s, w_nsa, o_diff_s, w_diff, proj_s, col_gb)
    y_sample = _mix2(merged_s, w_o, xs, final_norm_g).reshape(n_s, dec, d_model)
    segs_s, dk_s, dv_s = split_rows(proj_s, n_s, dec)
    segs_s[4] = win_k_s.reshape(1, n_s, wb, NSA_GROUPS, HEAD_DIM)
    segs_s[5] = win_v_s.reshape(1, n_s, wb, NSA_GROUPS, HEAD_DIM)

    return (y_prompt, y_sample, *segs_p, dk_p, dv_p, *segs_s, dk_s, dv_s)
```

```python
import functools
import math

import jax
import jax.numpy as jnp
from jax import lax
from jax.experimental import pallas as pl
from jax.experimental.pallas import tpu as pltpu

F32 = jnp.float32
BF16 = jnp.bfloat16

HEAD_DIM = 128
NSA_GROUPS = 4
NSA_REP = 4
NSA_HEADS = NSA_GROUPS * NSA_REP
CMP_BLOCK = 32
CMP_STRIDE = 16
CMP_HIDDEN = 2 * HEAD_DIM
SEL_BLOCK = 64
N_SELECT = 16
WINDOW = 512
DIFF_HEADS = 8
DIFF_VDIM = 2 * HEAD_DIM
ROPE_THETA = 10000.0
NORM_EPS = 1e-6
FORCE_SCORE = 1e4
NEG_INF = -1e30
SCALE = HEAD_DIM ** -0.5
PAGE_SIZE = 128

LANES = 128
NSA_Q_W = NSA_HEADS * HEAD_DIM
NSA_KV_W = NSA_GROUPS * HEAD_DIM
DIFF_W = DIFF_HEADS * DIFF_VDIM
PAD_ROWS = 128

COL_Q = 0
COL_Z_NSA = COL_Q + NSA_Q_W
COL_DQ = COL_Z_NSA + NSA_Q_W
COL_DK = COL_DQ + DIFF_W
COL_DV = COL_DK + DIFF_W
COL_Z_DIFF = COL_DV + DIFF_W
COL_KV = COL_Z_DIFF + DIFF_W
COL_GA = COL_KV + 6 * NSA_KV_W
PROJ_TN = 512
VMEM_LIMIT = 56 * 1024 * 1024


def _sigmoid(x):
    return 1.0 / (1.0 + jnp.exp(-x))


def _dot(a, b):
    return jnp.dot(a, b, preferred_element_type=F32)


def _dot_nt(a, b):
    return lax.dot_general(a, b, (((1,), (1,)), ((), ())), preferred_element_type=F32)


def _params(*sem):
    return pltpu.CompilerParams(dimension_semantics=sem, vmem_limit_bytes=VMEM_LIMIT)


def _rmsnorm_kernel(x_ref, g_ref, o_ref):
    x = x_ref[...]
    ms = jnp.mean(x * x, axis=-1, keepdims=True)
    o_ref[...] = (x * lax.rsqrt(ms + NORM_EPS) * g_ref[...]).astype(o_ref.dtype)


def _rmsnorm(x2d, g, out_dtype):
    m, d = x2d.shape
    tm = min(256, m)
    return pl.pallas_call(
        _rmsnorm_kernel,
        grid=(m // tm,),
        in_specs=[pl.BlockSpec((tm, d), lambda i: (i, 0)), pl.BlockSpec((1, d), lambda i: (0, 0))],
        out_specs=pl.BlockSpec((tm, d), lambda i: (i, 0)),
        out_shape=jax.ShapeDtypeStruct((m, d), out_dtype),
        compiler_params=_params("parallel"),
        name="rmsnorm",
    )(x2d, g.reshape(1, d))


def _proj_layout(d_model):
    col_gb = COL_GA + d_model
    col_gates = col_gb + d_model
    n_cols = col_gates + PROJ_TN
    return col_gb, col_gates, n_cols


def _pack_w_in(w, d_model):
    splits = (NSA_Q_W, 6 * NSA_KV_W, 3 * NSA_HEADS, NSA_Q_W, DIFF_W, DIFF_W, DIFF_W, DIFF_W, d_model, d_model)
    cuts = [sum(splits[:i + 1]) for i in range(len(splits) - 1)]
    q, kv, gates, z_nsa, dq, dk, dv, z_diff, g_a, g_b = jnp.split(w.T, cuts, axis=0)
    gates = gates.reshape(NSA_GROUPS, NSA_REP * 3, w.shape[0])
    gates = jnp.pad(gates, ((0, 0), (0, LANES - NSA_REP * 3), (0, 0))).reshape(NSA_GROUPS * LANES, w.shape[0])
    return jnp.concatenate([q, z_nsa, dq, dk, dv, z_diff, kv, g_a, g_b, gates], axis=0).astype(BF16)


def _proj_kernel(h_ref, w_ref, cos_ref, sin_ref, o_ref, *, rope_blocks, silu_blocks, sig_start, row_chunk):
    j = pl.program_id(1)

    def _in(blocks):
        pred = None
        for lo, hi in blocks:
            c = (j >= lo) & (j < hi)
            pred = c if pred is None else (pred | c)
        return pred

    is_rope = _in(rope_blocks)
    is_silu = _in(silu_blocks)
    is_sig = j >= sig_start
    w = w_ref[...]
    for r0 in range(0, o_ref.shape[0], row_chunk):
        rows = slice(r0, r0 + row_chunk)
        x = _dot_nt(h_ref[rows, :], w)
        a = jnp.where(is_rope, cos_ref[rows, :], 1.0)
        b = jnp.where(is_rope, sin_ref[rows, :], 0.0)
        lin = jnp.concatenate(
            [x[:, c * HEAD_DIM:(c + 1) * HEAD_DIM] * a
             + pltpu.roll(x[:, c * HEAD_DIM:(c + 1) * HEAD_DIM], HEAD_DIM // 2, 1) * b
             for c in range(x.shape[1] // HEAD_DIM)], axis=1)
        sg = _sigmoid(x)
        o_ref[rows, :] = jnp.where(is_sig, sg, jnp.where(is_silu, x * sg, lin))


def _project(h, w_packed, cos, sin):
    m, d = h.shape
    n_cols = w_packed.shape[0]
    tm = min(1024, m)
    tn = PROJ_TN
    b = lambda col: col // tn
    kv0 = b(COL_KV)
    rope_blocks = ((b(COL_Q), b(COL_Z_NSA)), (b(COL_DQ), b(COL_DV)),
                   (kv0, kv0 + 1), (kv0 + 2, kv0 + 3), (kv0 + 4, kv0 + 5))
    silu_blocks = ((b(COL_Z_NSA), b(COL_DQ)), (b(COL_Z_DIFF), b(COL_KV)))
    kern = functools.partial(_proj_kernel, rope_blocks=rope_blocks, silu_blocks=silu_blocks, sig_start=b(COL_GA),
                             row_chunk=min(256, tm))
    return pl.pallas_call(
        kern,
        grid=(m // tm, n_cols // tn),
        in_specs=[pl.BlockSpec((tm, d), lambda i, j: (i, 0)),
                  pl.BlockSpec((tn, d), lambda i, j: (j, 0)),
                  pl.BlockSpec((tm, HEAD_DIM), lambda i, j: (i, 0)),
                  pl.BlockSpec((tm, HEAD_DIM), lambda i, j: (i, 0))],
        out_specs=pl.BlockSpec((tm, tn), lambda i, j: (i, j)),
        out_shape=jax.ShapeDtypeStruct((m, n_cols), F32),
        compiler_params=_params("parallel", "arbitrary"),
        name="in_proj",
    )(h, w_packed, cos, sin)


def _rope_tables(pos):
    half = HEAD_DIM // 2
    inv = ROPE_THETA ** (-jnp.arange(half, dtype=F32) / half)
    ang = pos.astype(F32)[:, None] * inv[None, :]
    cos, sin = jnp.cos(ang), jnp.sin(ang)
    return jnp.concatenate([cos, cos], axis=1), jnp.concatenate([-sin, sin], axis=1)


def _pos_term_kernel(pe_ref, w1_ref, o_ref):
    pe = jnp.broadcast_to(pe_ref[0], (8, pe_ref.shape[2])).astype(BF16)
    o_ref[0] = _dot(pe, w1_ref[0])


def _pos_terms(pe_kv, w1_kv):
    kdim = pe_kv.shape[1]
    out = pl.pallas_call(
        _pos_term_kernel,
        grid=(2,),
        in_specs=[pl.BlockSpec((1, 1, kdim), lambda i: (i, 0, 0)),
                  pl.BlockSpec((1, kdim, CMP_HIDDEN), lambda i: (i, 0, 0))],
        out_specs=pl.BlockSpec((1, 8, CMP_HIDDEN), lambda i: (i, 0, 0)),
        out_shape=jax.ShapeDtypeStruct((2, 8, CMP_HIDDEN), F32),
        compiler_params=_params("arbitrary"),
        name="cmp_pos_term",
    )(pe_kv.reshape(2, 1, kdim), w1_kv)
    return out[:, 0:1, :]


def _finish_compress(first, second, pos, w2):
    rows = first.shape[0]
    shifted = pltpu.roll(second, rows - 1, 0)
    pre = first + shifted + pos
    hid = pre * _sigmoid(pre)
    return _dot(hid.astype(BF16), w2)


def _compress_prompt_kernel(x0_ref, x1_ref, x2_ref, x3_ref, w1_ref, pos_ref, w2_ref, o_ref, *, nh):
    x_refs = (x0_ref, x1_ref, x2_ref, x3_ref)
    acc = jnp.zeros((NSA_GROUPS * nh, 2 * CMP_HIDDEN), F32)
    for s in range(CMP_STRIDE):
        xs = jnp.concatenate([x[pl.ds(s, nh, stride=CMP_STRIDE), :] for x in x_refs], axis=0)
        acc = acc + _dot(xs.astype(BF16), w1_ref[0, s])
    out = _finish_compress(acc[:, :CMP_HIDDEN], acc[:, CMP_HIDDEN:], pos_ref[0], w2_ref[0])
    o_ref[0, 0] = out.reshape(NSA_GROUPS, nh, HEAD_DIM)


def _compress_prompt(proj, w1ab_kv, pos_kv, w2_kv, n, seq):
    nh = seq // CMP_STRIDE
    cb = COL_KV // HEAD_DIM
    kern = functools.partial(_compress_prompt_kernel, nh=nh)

    def group_spec(g):
        return pl.BlockSpec((seq, HEAD_DIM), lambda c, i: (i, cb + c * NSA_GROUPS + g))

    return pl.pallas_call(
        kern,
        grid=(2, n),
        in_specs=[group_spec(g) for g in range(NSA_GROUPS)]
        + [pl.BlockSpec((1, CMP_STRIDE, HEAD_DIM, 2 * CMP_HIDDEN), lambda c, i: (c, 0, 0, 0)),
           pl.BlockSpec((1, 1, CMP_HIDDEN), lambda c, i: (c, 0, 0)),
           pl.BlockSpec((1, CMP_HIDDEN, HEAD_DIM), lambda c, i: (c, 0, 0))],
        out_specs=pl.BlockSpec((1, 1, NSA_GROUPS, nh, HEAD_DIM), lambda c, i: (c, i, 0, 0, 0)),
        out_shape=jax.ShapeDtypeStruct((2, n, NSA_GROUPS, nh, HEAD_DIM), F32),
        compiler_params=_params("arbitrary", "arbitrary"),
        name="compress_prompt",
    )(proj, proj, proj, proj, w1ab_kv, pos_kv, w2_kv)


QK_SCALE = SCALE * math.log2(math.e)


def _scaled_q(q):
    return (q * QK_SCALE).astype(BF16)


def _cmp_probs(qs, kc, trow, nc):
    s = _dot_nt(qs, kc)
    cidx = lax.broadcasted_iota(jnp.int32, s.shape, 1)
    vmask = ((cidx * CMP_STRIDE + CMP_BLOCK) <= trow + 1) & (cidx < nc)
    sm = jnp.where(vmask, s, NEG_INF)
    e = jnp.exp2(sm - jnp.max(sm, axis=-1, keepdims=True))
    p = e / jnp.sum(e, axis=-1, keepdims=True)
    return jnp.where(vmask, p, 0.0)


def _block_scores(psum, nblk):
    nc_rows = psum.shape[1]
    cc = lax.broadcasted_iota(jnp.int32, (nc_rows, LANES), 0) * CMP_STRIDE
    jj = lax.broadcasted_iota(jnp.int32, (nc_rows, LANES), 1)
    cover = (cc < jj * SEL_BLOCK + SEL_BLOCK) & (cc + CMP_BLOCK > jj * SEL_BLOCK) & (jj < nblk)
    cover = jnp.where(cover, 1.0, 0.0).astype(BF16)
    hi = psum.astype(BF16)
    r1 = psum - hi.astype(F32)
    mid = r1.astype(BF16)
    lo = (r1 - mid.astype(F32)).astype(BF16)
    return _dot(hi, cover) + _dot(mid, cover) + _dot(lo, cover)


def _rank_select(ranked, nblk, nsel):
    rt = ranked.T[0:nblk]
    ji = lax.broadcasted_iota(jnp.int32, rt.shape, 0)
    rank = jnp.zeros(rt.shape, F32)
    for i in range(nblk):
        row = rt[i:i + 1, :]
        beats = (row > rt) | ((row == rt) & (ji > i))
        rank = rank + jnp.where(beats, 1.0, 0.0)
    sel_t = jnp.where(rank < nsel, 1.0, 0.0)
    sel_t = jnp.concatenate([sel_t, jnp.zeros((LANES - nblk, rt.shape[1]), F32)], axis=0)
    return sel_t.T


def _expand_blocks(sel01, n_keys):
    jj = lax.broadcasted_iota(jnp.int32, (LANES, n_keys), 0)
    kk = lax.broadcasted_iota(jnp.int32, (LANES, n_keys), 1)
    expand = jnp.where((kk // SEL_BLOCK) == jj, 1.0, 0.0).astype(BF16)
    return _dot(sel01, expand)


def _softmax_weights(s):
    e = jnp.exp2(s - jnp.max(s, axis=-1, keepdims=True))
    return e, jnp.sum(e, axis=-1, keepdims=True)


def _masked_attend(qs, k, v, mask):
    e, l = _softmax_weights(jnp.where(mask, _dot_nt(qs, k), NEG_INF))
    return _dot(e.astype(BF16), v) / l


def _biased_attend(qs, k, v, bias):
    e, l = _softmax_weights(_dot_nt(qs, k) + bias)
    return _dot(e.astype(BF16), v) / l


def _online_update(m, l, acc, s, v):
    m_new = jnp.maximum(m, jnp.max(s, axis=-1, keepdims=True))
    alpha = jnp.exp2(m - m_new)
    e = jnp.exp2(s - m_new)
    return m_new, alpha * l + jnp.sum(e, axis=-1, keepdims=True), alpha * acc + _dot(e.astype(BF16), v)


def _nsa_prompt_kernel(q_ref, kc_ref, vc_ref, ks_ref, vs_ref, kw_ref, vw_ref, gate_ref, z_ref, o_ref,
                       ksb, vsb, kwb, vwb, osel, *, tq, seq, nc, nblk, nsel, kstep):
    qb = pl.program_id(2)
    q0 = qb * tq

    @pl.when(qb == 0)
    def _():
        ksb[...] = ks_ref[...].astype(BF16)
        vsb[...] = vs_ref[...].astype(BF16)
        kwb[...] = kw_ref[...].astype(BF16)
        vwb[...] = vw_ref[...].astype(BF16)

    q = q_ref[...]
    qs = _scaled_q(jnp.concatenate([q[:, r * HEAD_DIM:(r + 1) * HEAD_DIM] for r in range(NSA_REP)], axis=0))

    kc = kc_ref[0, 0, 0].astype(BF16)
    vc = vc_ref[0, 0, 0].astype(BF16)
    ncp = kc.shape[0]
    trow = (lax.broadcasted_iota(jnp.int32, (NSA_REP * tq, ncp), 0) & (tq - 1)) + q0
    p = _cmp_probs(qs, kc, trow, nc)
    o_cmp = _dot(p.astype(BF16), vc)

    psum = p[0:tq] + p[tq:2 * tq] + p[2 * tq:3 * tq] + p[3 * tq:4 * tq]
    score = _block_scores(psum, nblk)
    jl = lax.broadcasted_iota(jnp.int32, (tq, LANES), 1)
    tl = lax.broadcasted_iota(jnp.int32, (tq, LANES), 0) + q0
    cur = tl // SEL_BLOCK
    forced = (jl == 0) | (jl == cur) | (jl == cur - 1)
    ranked = jnp.where(forced, FORCE_SCORE, jnp.where(jl <= cur, score, -1.0))
    sel01 = jnp.where(jl <= cur, _rank_select(ranked, nblk, nsel), 0.0).astype(BF16)

    for v in range(seq // kstep):
        ext = (v + 1) * kstep

        @pl.when(q0 // kstep == v)
        def _(ext=ext):
            key_sel = _expand_blocks(sel01, ext)
            kpos = lax.broadcasted_iota(jnp.int32, (tq, ext), 1)
            tpos = lax.broadcasted_iota(jnp.int32, (tq, ext), 0) + q0
            bias = jnp.where((key_sel > 0.5) & (kpos <= tpos), 0.0, NEG_INF)
            ks = ksb[0:ext, :]
            vs = vsb[0:ext, :]
            for r in range(NSA_REP):
                osel[r * tq:(r + 1) * tq, :] = _biased_attend(qs[r * tq:(r + 1) * tq], ks, vs, bias)

    span = min(tq + WINDOW, seq)
    start = pl.multiple_of(jnp.clip(q0 + tq - span, 0, seq - span), tq)
    wk = lax.broadcasted_iota(jnp.int32, (tq, span), 1) + start
    wd = (lax.broadcasted_iota(jnp.int32, (tq, span), 0) + q0) - wk
    win_bias = jnp.where((wd >= 0) & (wd < WINDOW), 0.0, NEG_INF)
    kw = kwb[pl.ds(start, span), :]
    vw = vwb[pl.ds(start, span), :]
    gt = gate_ref[...]
    z = z_ref[...]
    for r in range(NSA_REP):
        o_sel = osel[r * tq:(r + 1) * tq, :]
        o_win = _biased_attend(qs[r * tq:(r + 1) * tq], kw, vw, win_bias)
        o = (gt[:, 3 * r:3 * r + 1] * o_cmp[r * tq:(r + 1) * tq] + gt[:, 3 * r + 1:3 * r + 2] * o_sel
             + gt[:, 3 * r + 2:3 * r + 3] * o_win)
        sl = slice(r * HEAD_DIM, (r + 1) * HEAD_DIM)
        o_ref[:, sl] = (o * z[:, sl]).astype(o_ref.dtype)


def _nsa_prompt(proj, kvc, col_gates, n, seq):
    tq = 128
    nqb = seq // tq
    nh = kvc.shape[3]
    nblk = seq // SEL_BLOCK
    kern = functools.partial(_nsa_prompt_kernel, tq=tq, seq=seq, nc=nh - 1, nblk=nblk, nsel=min(N_SELECT, nblk),
                             kstep=256)
    gw = NSA_REP * HEAD_DIM
    kvb = COL_KV // HEAD_DIM

    def kv_spec(which):
        return pl.BlockSpec((seq, HEAD_DIM), lambda i, g, qb: (i, kvb + which * NSA_GROUPS + g))

    return pl.pallas_call(
        kern,
        grid=(n, NSA_GROUPS, nqb),
        in_specs=[pl.BlockSpec((tq, gw), lambda i, g, qb: (i * nqb + qb, COL_Q // gw + g)),
                  pl.BlockSpec((1, 1, 1, nh, HEAD_DIM), lambda i, g, qb: (0, i, g, 0, 0)),
                  pl.BlockSpec((1, 1, 1, nh, HEAD_DIM), lambda i, g, qb: (1, i, g, 0, 0)),
                  kv_spec(2), kv_spec(3), kv_spec(4), kv_spec(5),
                  pl.BlockSpec((tq, LANES), lambda i, g, qb: (i * nqb + qb, col_gates // LANES + g)),
                  pl.BlockSpec((tq, gw), lambda i, g, qb: (i * nqb + qb, COL_Z_NSA // gw + g))],
        out_specs=pl.BlockSpec((tq, gw), lambda i, g, qb: (i * nqb + qb, g)),
        out_shape=jax.ShapeDtypeStruct((n * seq, NSA_Q_W), BF16),
        scratch_shapes=[pltpu.VMEM((seq, HEAD_DIM), BF16)] * 4 + [pltpu.VMEM((NSA_REP * tq, HEAD_DIM), F32)],
        compiler_params=_params("arbitrary", "arbitrary", "arbitrary"),
        name="nsa_prompt",
    )(proj, kvc, kvc, proj, proj, proj, proj, proj, proj)


def _lambda(lq1_ref, lk1_ref, lq2_ref, lk2_ref, lam_init):
    a = jnp.sum(lq1_ref[...] * lk1_ref[...], axis=-1, keepdims=True)
    b = jnp.sum(lq2_ref[...] * lk2_ref[...], axis=-1, keepdims=True)
    return jnp.exp(a) - jnp.exp(b) + lam_init


def _diff_finish(o, g, z, lam_init):
    ms = jnp.mean(o * o, axis=-1, keepdims=True)
    return o * lax.rsqrt(ms + NORM_EPS) * g * (1.0 - lam_init) * z


def _diff_prompt_kernel(lq1_ref, lk1_ref, lq2_ref, lk2_ref, q_ref, k_ref, v_ref, z_ref, g_ref, o_ref,
                        kb, vb, *, tq, seq, lam_init, kstep):
    qb = pl.program_id(2)
    q0 = qb * tq

    @pl.when(qb == 0)
    def _():
        kb[...] = k_ref[...].astype(BF16)
        vb[...] = v_ref[...].astype(BF16)

    lam = _lambda(lq1_ref, lk1_ref, lq2_ref, lk2_ref, lam_init)
    q = _scaled_q(q_ref[...])

    for v in range(seq // kstep):
        ext = (v + 1) * kstep

        @pl.when(q0 // kstep == v)
        def _(ext=ext):
            causal = (lax.broadcasted_iota(jnp.int32, (tq, ext), 1)
                      <= lax.broadcasted_iota(jnp.int32, (tq, ext), 0) + q0)
            bias = jnp.where(causal, 0.0, NEG_INF)
            k = kb[0:ext, :]
            maps = []
            for c in range(2):
                sl = slice(c * HEAD_DIM, (c + 1) * HEAD_DIM)
                e, l = _softmax_weights(_dot_nt(q[:, sl], k[:, sl]) + bias)
                maps.append(e / l)
            w = maps[0] - lam * maps[1]
            o = _dot(w.astype(BF16), vb[0:ext, :])
            o_ref[...] = _diff_finish(o, g_ref[...], z_ref[...], lam_init).astype(o_ref.dtype)


def _diff_prompt(proj, lams, diff_g, lam_init, n, seq):
    tq = 256
    nqb = seq // tq
    kern = functools.partial(_diff_prompt_kernel, tq=tq, seq=seq, lam_init=lam_init, kstep=256)
    hw = DIFF_VDIM
    lam_spec = pl.BlockSpec((1, HEAD_DIM), lambda i, h, qb: (0, 0))
    return pl.pallas_call(
        kern,
        grid=(n, DIFF_HEADS, nqb),
        in_specs=[lam_spec, lam_spec, lam_spec, lam_spec,
                  pl.BlockSpec((tq, hw), lambda i, h, qb: (i * nqb + qb, COL_DQ // hw + h)),
                  pl.BlockSpec((seq, hw), lambda i, h, qb: (i, COL_DK // hw + h)),
                  pl.BlockSpec((seq, hw), lambda i, h, qb: (i, COL_DV // hw + h)),
                  pl.BlockSpec((tq, hw), lambda i, h, qb: (i * nqb + qb, COL_Z_DIFF // hw + h)),
                  pl.BlockSpec((1, hw), lambda i, h, qb: (0, 0))],
        out_specs=pl.BlockSpec((tq, hw), lambda i, h, qb: (i * nqb + qb, h)),
        out_shape=jax.ShapeDtypeStruct((n * seq, DIFF_W), BF16),
        scratch_shapes=[pltpu.VMEM((seq, hw), BF16)] * 2,
        compiler_params=_params("arbitrary", "arbitrary", "arbitrary"),
        name="diff_prompt",
    )(*lams, proj, proj, proj, proj, diff_g)


def _mix1_kernel(a_ref, wa_ref, b_ref, wb_ref, ga_ref, gb_ref, o_ref):
    pa = _dot(a_ref[...].astype(BF16), wa_ref[...])
    pb = _dot(b_ref[...].astype(BF16), wb_ref[...])
    o_ref[...] = (ga_ref[...] * pa + gb_ref[...] * pb).astype(o_ref.dtype)


def _mix1(o_nsa, w_nsa, o_diff, w_diff, proj, col_gb):
    m = o_nsa.shape[0]
    d_model = w_nsa.shape[1]
    tm = min(1024, m)
    tn = 512
    return pl.pallas_call(
        _mix1_kernel,
        grid=(m // tm, d_model // tn),
        in_specs=[pl.BlockSpec((tm, NSA_Q_W), lambda i, j: (i, 0)),
                  pl.BlockSpec((NSA_Q_W, tn), lambda i, j: (0, j)),
                  pl.BlockSpec((tm, DIFF_W), lambda i, j: (i, 0)),
                  pl.BlockSpec((DIFF_W, tn), lambda i, j: (0, j)),
                  pl.BlockSpec((tm, tn), lambda i, j: (i, COL_GA // tn + j)),
                  pl.BlockSpec((tm, tn), lambda i, j: (i, col_gb // tn + j))],
        out_specs=pl.BlockSpec((tm, tn), lambda i, j: (i, j)),
        out_shape=jax.ShapeDtypeStruct((m, d_model), BF16),
        compiler_params=_params("parallel", "arbitrary"),
        name="mix_gated_proj",
    )(o_nsa, w_nsa, o_diff, w_diff, proj, proj)


def _mix2_kernel(m_ref, w_ref, x_ref, g_ref, o_ref, *, n_chunk):
    k = pl.program_id(1)
    d_model = o_ref.shape[1]
    cols = [slice(n0, n0 + n_chunk) for n0 in range(0, d_model, n_chunk)]

    @pl.when(k == 0)
    def _():
        o_ref[...] = x_ref[...]

    mm = m_ref[...]
    for sl in cols:
        o_ref[:, sl] += _dot(mm, w_ref[:, sl])

    @pl.when(k == pl.num_programs(1) - 1)
    def _():
        ss = jnp.zeros((o_ref.shape[0], 1), F32)
        for sl in cols:
            y = o_ref[:, sl]
            ss = ss + jnp.sum(y * y, axis=-1, keepdims=True)
        r = lax.rsqrt(ss * (1.0 / d_model) + NORM_EPS)
        for sl in cols:
            o_ref[:, sl] = o_ref[:, sl] * r * g_ref[:, sl]


def _mix2(merged, w_out, x2d, final_g):
    m, d_model = x2d.shape
    tm = min(512, m)
    tk = 512
    return pl.pallas_call(
        functools.partial(_mix2_kernel, n_chunk=1024),
        grid=(m // tm, d_model // tk),
        in_specs=[pl.BlockSpec((tm, tk), lambda i, k: (i, k)),
                  pl.BlockSpec((tk, d_model), lambda i, k: (k, 0)),
                  pl.BlockSpec((tm, d_model), lambda i, k: (i, 0)),
                  pl.BlockSpec((1, d_model), lambda i, k: (0, 0))],
        out_specs=pl.BlockSpec((tm, d_model), lambda i, k: (i, 0)),
        out_shape=jax.ShapeDtypeStruct((m, d_model), F32),
        compiler_params=_params("parallel", "arbitrary"),
        name="out_proj_residual_norm",
    )(merged, w_out, x2d, final_g.reshape(1, d_model))


def _cmp_partial_kernel(pt_ref, *refs, nbs):
    del pt_ref
    x_refs, w_ref, o_ref = refs[:nbs], refs[nbs], refs[nbs + 1]
    hpp = PAGE_SIZE // CMP_STRIDE
    acc = jnp.zeros((nbs * NSA_GROUPS * hpp, 2 * CMP_HIDDEN), F32)

    def token_rows(s):
        return jnp.concatenate([x[pl.ds(NSA_GROUPS * s + g, hpp, stride=SLAB_PITCH), :]
                                for x in flat for g in range(NSA_GROUPS)], axis=0)

    flat = [x.reshape(hpp * SLAB_PITCH, HEAD_DIM) for x in x_refs]

    for s in range(0, CMP_STRIDE, 2):
        xs = jnp.concatenate([token_rows(s), token_rows(s + 1)], axis=1).astype(BF16)
        acc = acc + _dot(xs, w_ref[s // 2])
    o_ref[...] = acc.reshape(nbs, NSA_GROUPS, hpp, 2 * CMP_HIDDEN)


SLAB_ROWS = CMP_STRIDE * NSA_GROUPS
SLAB_PITCH = SLAB_ROWS + 8


def _cmp_partials(cache, page_flat, w1ab, n, n_pages):
    pool = cache.shape[0]
    hpp = PAGE_SIZE // CMP_STRIDE
    view = cache.reshape(pool, hpp, SLAB_ROWS, HEAD_DIM)
    w_pairs = w1ab.reshape(CMP_STRIDE // 2, 2 * HEAD_DIM, 2 * CMP_HIDDEN)
    nbs = min(16, n)
    kern = functools.partial(_cmp_partial_kernel, nbs=nbs)

    def page_spec(i):
        return pl.BlockSpec((1, hpp, SLAB_PITCH, HEAD_DIM),
                            lambda nb, p, pt: (pt[(nb * nbs + i) * n_pages + p], 0, 0, 0))

    return pl.pallas_call(
        kern,
        grid_spec=pltpu.PrefetchScalarGridSpec(
            num_scalar_prefetch=1,
            grid=(n // nbs, n_pages),
            in_specs=[page_spec(i) for i in range(nbs)]
            + [pl.BlockSpec((CMP_STRIDE // 2, 2 * HEAD_DIM, 2 * CMP_HIDDEN), lambda nb, p, pt: (0, 0, 0))],
            out_specs=pl.BlockSpec((nbs, NSA_GROUPS, hpp, 2 * CMP_HIDDEN), lambda nb, p, pt: (nb, 0, p, 0)),
        ),
        out_shape=jax.ShapeDtypeStruct((n, NSA_GROUPS, n_pages * hpp, 2 * CMP_HIDDEN), F32),
        compiler_params=_params("parallel", "arbitrary"),
        name="cmp_partials_sample",
    )(page_flat, *([view] * nbs), w_pairs)


def _cmp_sample_kernel(fsk_ref, fsv_ref, pos_ref, w2_ref, q_ref, ocmp_ref, sel_ref,
                       *, nb2, nh, nc, past_len, dec, nblk, nsel):
    rows = nb2 * NSA_GROUPS * nh

    def compress(fs_ref, c):
        fs = fs_ref[...].reshape(rows, 2 * CMP_HIDDEN)
        return _finish_compress(fs[:, :CMP_HIDDEN], fs[:, CMP_HIDDEN:], pos_ref[c], w2_ref[c])

    kc = compress(fsk_ref, 0).astype(BF16)
    vc = compress(fsv_ref, 1).astype(BF16)
    trow = (lax.broadcasted_iota(jnp.int32, (NSA_REP * dec, nh), 0) & (dec - 1)) + past_len
    scores = []
    for i in range(nb2):
        qi = q_ref[i * dec:(i + 1) * dec, :]
        for g in range(NSA_GROUPS):
            base = (i * NSA_GROUPS + g) * nh
            c0 = g * NSA_REP * HEAD_DIM
            qg = _scaled_q(jnp.concatenate([qi[:, c0 + r * HEAD_DIM:c0 + (r + 1) * HEAD_DIM] for r in range(NSA_REP)],
                                           axis=0))
            p = _cmp_probs(qg, kc[base:base + nh], trow, nc)
            o = _dot(p.astype(BF16), vc[base:base + nh])
            for r in range(NSA_REP):
                ocmp_ref[i * dec:(i + 1) * dec, c0 + r * HEAD_DIM:c0 + (r + 1) * HEAD_DIM] = o[r * dec:(r + 1) * dec]
            psum = p[0:dec] + p[dec:2 * dec] + p[2 * dec:3 * dec] + p[3 * dec:4 * dec]
            scores.append(_block_scores(psum, nblk))
    score = jnp.concatenate(scores, axis=0)
    jl = lax.broadcasted_iota(jnp.int32, score.shape, 1)
    ranked = jnp.where((jl == 0) | (jl == nblk - 1), FORCE_SCORE, score)
    sel_ref[...] = _rank_select(ranked, nblk, nsel).reshape(nb2, NSA_GROUPS, dec, LANES)


def _cmp_sample(fs_k, fs_v, pos_kv, w2_kv, proj_s, n, dec, past_len):
    nh = fs_k.shape[2]
    nb2 = min(4, n)
    nblk = past_len // SEL_BLOCK
    kern = functools.partial(_cmp_sample_kernel, nb2=nb2, nh=nh, nc=nh - 1, past_len=past_len, dec=dec,
                             nblk=nblk, nsel=min(N_SELECT - 1, nblk))
    fs_spec = pl.BlockSpec((nb2, NSA_GROUPS, nh, 2 * CMP_HIDDEN), lambda i: (i, 0, 0, 0))
    return pl.pallas_call(
        kern,
        grid=(n // nb2,),
        in_specs=[fs_spec, fs_spec,
                  pl.BlockSpec((2, 1, CMP_HIDDEN), lambda i: (0, 0, 0)),
                  pl.BlockSpec((2, CMP_HIDDEN, HEAD_DIM), lambda i: (0, 0, 0)),
                  pl.BlockSpec((nb2 * dec, NSA_Q_W), lambda i: (i, 0))],
        out_specs=[pl.BlockSpec((nb2 * dec, NSA_Q_W), lambda i: (i, 0)),
                   pl.BlockSpec((nb2, NSA_GROUPS, dec, LANES), lambda i: (i, 0, 0, 0))],
        out_shape=[jax.ShapeDtypeStruct((n * dec, NSA_Q_W), F32),
                   jax.ShapeDtypeStruct((n, NSA_GROUPS, dec, LANES), F32)],
        compiler_params=_params("parallel"),
        name="cmp_attend_select_sample",
    )(fs_k, fs_v, pos_kv, w2_kv, proj_s)


def _pad_new(x):
    return jnp.concatenate([x, jnp.zeros((PAD_ROWS - x.shape[0], x.shape[1]), x.dtype)], axis=0).astype(BF16)


def _nsa_sample_kernel(pt_ref, *refs, n_pages, past_len, dec, wb):
    del pt_ref
    ksel_pages = refs[:n_pages]
    vsel_pages = refs[n_pages:2 * n_pages]
    (bufk_ref, bufv_ref, q_ref, ksn_ref, vsn_ref, kwn_ref, vwn_ref, ocmp_ref, gate_ref, z_ref, sel_ref,
     o_ref, nbk_ref, nbv_ref) = refs[2 * n_pages:]
    rows = NSA_REP * dec
    t_of_row = lambda shape: lax.broadcasted_iota(jnp.int32, shape, 0) & (dec - 1)

    def group_rows(ref, g, n_tok):
        return ref[0, pl.ds(g, n_tok, stride=NSA_GROUPS), :]

    keep_rows = (wb - dec) * NSA_GROUPS
    for buf_ref, new_ref, out_ref in ((bufk_ref, kwn_ref, nbk_ref), (bufv_ref, vwn_ref, nbv_ref)):
        out_ref[0, 0:keep_rows, :] = buf_ref[0, dec * NSA_GROUPS:wb * NSA_GROUPS, :]
        for g in range(NSA_GROUPS):
            out_ref[0, pl.ds(keep_rows + g, dec, stride=NSA_GROUPS), :] = new_ref[:, g * HEAD_DIM:(g + 1) * HEAD_DIM]

    jn = lax.broadcasted_iota(jnp.int32, (rows, PAD_ROWS), 1)
    new_ok = jnp.where((jn < dec) & (jn <= t_of_row((rows, PAD_ROWS))), 1.0, 0.0)
    n_win = wb + PAD_ROWS
    wi = lax.broadcasted_iota(jnp.int32, (rows, n_win), 1)
    wkpos = jnp.where(wi < wb, past_len - wb + wi, past_len + wi - wb)
    wd = (past_len + t_of_row((rows, n_win))) - wkpos
    win_mask = (wd >= 0) & (wd < WINDOW) & (wi < wb + dec)

    q = q_ref[...]
    gt = gate_ref[...]
    z = z_ref[...]
    for g in range(NSA_GROUPS):
        sl = slice(g * HEAD_DIM, (g + 1) * HEAD_DIM)
        c0 = g * NSA_REP * HEAD_DIM
        qg = _scaled_q(jnp.concatenate([q[:, c0 + r * HEAD_DIM:c0 + (r + 1) * HEAD_DIM] for r in range(NSA_REP)],
                                       axis=0))
        kg = jnp.concatenate([group_rows(pg, g, PAGE_SIZE).astype(BF16) for pg in ksel_pages]
                             + [_pad_new(ksn_ref[:, sl])], axis=0)
        vg = jnp.concatenate([group_rows(pg, g, PAGE_SIZE).astype(BF16) for pg in vsel_pages]
                             + [_pad_new(vsn_ref[:, sl])], axis=0)
        key_sel = _expand_blocks(sel_ref[0, g].astype(BF16), past_len)
        ok = jnp.concatenate([jnp.concatenate([key_sel] * NSA_REP, axis=0), new_ok], axis=1)
        o_sel = _masked_attend(qg, kg, vg, ok > 0.5)
        kw = jnp.concatenate([group_rows(bufk_ref, g, wb).astype(BF16), _pad_new(kwn_ref[:, sl])], axis=0)
        vw = jnp.concatenate([group_rows(bufv_ref, g, wb).astype(BF16), _pad_new(vwn_ref[:, sl])], axis=0)
        o_win = _masked_attend(qg, kw, vw, win_mask)
        for r in range(NSA_REP):
            cs = slice(c0 + r * HEAD_DIM, c0 + (r + 1) * HEAD_DIM)
            rs = slice(r * dec, (r + 1) * dec)
            gi = g * LANES + 3 * r
            o = (gt[:, gi:gi + 1] * ocmp_ref[:, cs] + gt[:, gi + 1:gi + 2] * o_sel[rs]
                 + gt[:, gi + 2:gi + 3] * o_win[rs])
            o_ref[:, cs] = o * z[:, cs]


def _nsa_sample(cache_k, cache_v, buf_k, buf_v, page_flat, proj_s, o_cmp, sel, col_gates, n, dec, n_pages):
    pool = cache_k.shape[0]
    wb = buf_k.shape[1]
    past_len = n_pages * PAGE_SIZE
    kview = cache_k.reshape(pool, PAGE_SIZE * NSA_GROUPS, HEAD_DIM)
    vview = cache_v.reshape(pool, PAGE_SIZE * NSA_GROUPS, HEAD_DIM)
    bk = buf_k.reshape(n, wb * NSA_GROUPS, HEAD_DIM)
    bv = buf_v.reshape(n, wb * NSA_GROUPS, HEAD_DIM)
    kern = functools.partial(_nsa_sample_kernel, n_pages=n_pages, past_len=past_len, dec=dec, wb=wb)
    kvb = COL_KV // NSA_KV_W

    def page_spec(p):
        return pl.BlockSpec((1, PAGE_SIZE * NSA_GROUPS, HEAD_DIM), lambda i, pt: (pt[i * n_pages + p], 0, 0))

    def row_spec(width, col_block):
        return pl.BlockSpec((dec, width), lambda i, pt: (i, col_block))

    buf_spec = pl.BlockSpec((1, wb * NSA_GROUPS, HEAD_DIM), lambda i, pt: (i, 0, 0))
    in_specs = ([page_spec(p) for p in range(n_pages)] + [page_spec(p) for p in range(n_pages)]
                + [buf_spec, buf_spec,
                   row_spec(NSA_Q_W, COL_Q // NSA_Q_W),
                   row_spec(NSA_KV_W, kvb + 2), row_spec(NSA_KV_W, kvb + 3),
                   row_spec(NSA_KV_W, kvb + 4), row_spec(NSA_KV_W, kvb + 5),
                   row_spec(NSA_Q_W, 0),
                   row_spec(NSA_GROUPS * LANES, col_gates // (NSA_GROUPS * LANES)),
                   row_spec(NSA_Q_W, COL_Z_NSA // NSA_Q_W),
                   pl.BlockSpec((1, NSA_GROUPS, dec, LANES), lambda i, pt: (i, 0, 0, 0))])
    return pl.pallas_call(
        kern,
        grid_spec=pltpu.PrefetchScalarGridSpec(
            num_scalar_prefetch=1,
            grid=(n,),
            in_specs=in_specs,
            out_specs=[row_spec(NSA_Q_W, 0), buf_spec, buf_spec],
        ),
        out_shape=[jax.ShapeDtypeStruct((n * dec, NSA_Q_W), F32),
                   jax.ShapeDtypeStruct((n, wb * NSA_GROUPS, HEAD_DIM), F32),
                   jax.ShapeDtypeStruct((n, wb * NSA_GROUPS, HEAD_DIM), F32)],
        compiler_params=_params("parallel"),
        name="nsa_sample",
    )(page_flat, *([kview] * n_pages), *([vview] * n_pages), bk, bv,
      proj_s, proj_s, proj_s, proj_s, proj_s, o_cmp, proj_s, proj_s, sel)


def _diff_sample_kernel(pt_ref, *refs, ppc, dec, lam_init):
    del pt_ref
    k_pages = refs[:ppc]
    v_pages = refs[ppc:2 * ppc]
    (lq1_ref, lk1_ref, lq2_ref, lk2_ref, q_ref, kn_ref, vn_ref, z_ref, g_ref, o_ref,
     m_sc, l_sc, acc_sc) = refs[2 * ppc:]
    j = pl.program_id(1)
    last = pl.num_programs(1) - 1

    @pl.when(j == 0)
    def _():
        m_sc[...] = jnp.full(m_sc.shape, NEG_INF, F32)
        l_sc[...] = jnp.zeros(l_sc.shape, F32)
        acc_sc[...] = jnp.zeros(acc_sc.shape, F32)

    hh = 2 * DIFF_HEADS
    nrow = hh * dec
    half = DIFF_HEADS * dec
    slab = PAGE_SIZE * DIFF_HEADS

    def rows_hd(x, width, c):
        return jnp.concatenate([x[:, h * DIFF_VDIM + c * width:h * DIFF_VDIM + (c + 1) * width]
                                for h in range(DIFF_HEADS)], axis=0)

    q = q_ref[...]
    qmap = [_scaled_q(rows_hd(q, HEAD_DIM, c)) for c in range(2)]
    ri = lax.broadcasted_iota(jnp.int32, (half, slab), 0)
    li = lax.broadcasted_iota(jnp.int32, (half, slab), 1)
    past_ok = (li & (DIFF_HEADS - 1)) == (ri // dec)

    state = [(m_sc[c * half:(c + 1) * half, 0:1], l_sc[c * half:(c + 1) * half, 0:1],
              acc_sc[c * half:(c + 1) * half, :]) for c in range(2)]
    for kp, vp in zip(k_pages, v_pages):
        vv = jnp.concatenate([vp[0, :, 0:DIFF_HEADS, :].reshape(slab, HEAD_DIM),
                              vp[0, :, DIFF_HEADS:hh, :].reshape(slab, HEAD_DIM)], axis=1).astype(BF16)
        for c in range(2):
            xb = kp[0, :, c * DIFF_HEADS:(c + 1) * DIFF_HEADS, :].reshape(slab, HEAD_DIM).astype(BF16)
            s = jnp.where(past_ok, _dot_nt(qmap[c], xb), NEG_INF)
            state[c] = _online_update(*state[c], s, vv)

    rn = lax.broadcasted_iota(jnp.int32, (half, half), 0)
    ln = lax.broadcasted_iota(jnp.int32, (half, half), 1)
    gate = jnp.where(j == last, 0, -half)
    new_ok = ((ln // dec) == (rn // dec)) & ((ln & (dec - 1)) <= (rn & (dec - 1)) + gate)
    vn = rows_hd(vn_ref[...], DIFF_VDIM, 0).astype(BF16)
    kn = kn_ref[...]
    for c in range(2):
        s = jnp.where(new_ok, _dot_nt(qmap[c], rows_hd(kn, HEAD_DIM, c).astype(BF16)), NEG_INF)
        m, l, acc = _online_update(*state[c], s, vn)
        m_sc[c * half:(c + 1) * half, :] = jnp.broadcast_to(m, (half, LANES))
        l_sc[c * half:(c + 1) * half, :] = jnp.broadcast_to(l, (half, LANES))
        acc_sc[c * half:(c + 1) * half, :] = acc

    @pl.when(j == last)
    def _():
        lam = _lambda(lq1_ref, lk1_ref, lq2_ref, lk2_ref, lam_init)
        o = acc_sc[...] / l_sc[:, 0:1]
        for h in range(DIFF_HEADS):
            vsl = slice(h * DIFF_VDIM, (h + 1) * DIFF_VDIM)
            o1 = o[h * dec:(h + 1) * dec]
            o2 = o[half + h * dec:half + (h + 1) * dec]
            o_ref[:, vsl] = _diff_finish(o1 - lam * o2, g_ref[...], z_ref[:, vsl], lam_init)


def _diff_cache_view(cache):
    pool = cache.shape[0]
    v = cache.reshape(pool, PAGE_SIZE, DIFF_HEADS, 2, HEAD_DIM).transpose(0, 1, 3, 2, 4)
    return v.reshape(pool, PAGE_SIZE, 2 * DIFF_HEADS, HEAD_DIM)


def _diff_sample(cache_k, cache_v, page_flat, proj_s, lams, diff_g, lam_init, n, dec, n_pages):
    kview = _diff_cache_view(cache_k)
    vview = _diff_cache_view(cache_v)
    n_chunks = 2 if n_pages % 2 == 0 else 1
    ppc = n_pages // n_chunks
    rows = 2 * DIFF_HEADS * dec
    kern = functools.partial(_diff_sample_kernel, ppc=ppc, dec=dec, lam_init=lam_init)

    def page_spec(p):
        return pl.BlockSpec((1, PAGE_SIZE, 2 * DIFF_HEADS, HEAD_DIM),
                            lambda i, j, pt: (pt[i * n_pages + j * ppc + p], 0, 0, 0))

    def row_spec(col0):
        return pl.BlockSpec((dec, DIFF_W), lambda i, j, pt: (i, col0 // DIFF_W))

    lam_spec = pl.BlockSpec((1, HEAD_DIM), lambda i, j, pt: (0, 0))
    in_specs = ([page_spec(p) for p in range(ppc)] + [page_spec(p) for p in range(ppc)]
                + [lam_spec] * 4
                + [row_spec(COL_DQ), row_spec(COL_DK), row_spec(COL_DV), row_spec(COL_Z_DIFF),
                   pl.BlockSpec((1, DIFF_VDIM), lambda i, j, pt: (0, 0))])
    return pl.pallas_call(
        kern,
        grid_spec=pltpu.PrefetchScalarGridSpec(
            num_scalar_prefetch=1,
            grid=(n, n_chunks),
            in_specs=in_specs,
            out_specs=pl.BlockSpec((dec, DIFF_W), lambda i, j, pt: (i, 0)),
            scratch_shapes=[pltpu.VMEM((rows, LANES), F32),
                            pltpu.VMEM((rows, LANES), F32),
                            pltpu.VMEM((rows, DIFF_VDIM), F32)],
        ),
        out_shape=jax.ShapeDtypeStruct((n * dec, DIFF_W), F32),
        compiler_params=_params("parallel", "arbitrary"),
        name="diff_sample",
    )(page_flat, *([kview] * ppc), *([vview] * ppc), *lams, proj_s, proj_s, proj_s, proj_s, diff_g)


def kernel(x_prompt, x_sample, cache_nsa_cmp_k, cache_nsa_cmp_v, cache_nsa_sel_k, cache_nsa_sel_v,
           state_nsa_win_k, state_nsa_win_v, cache_diff_k, cache_diff_v, page_table,
           norm_g, w_in, cmp_k_w1, cmp_k_pe, cmp_k_w2, cmp_v_w1, cmp_v_pe, cmp_v_w2,
           diff_lq1, diff_lk1, diff_lq2, diff_lk2, diff_norm_g, w_proj_nsa, w_proj_diff, w_out,
           final_norm_g):
    n_p, seq, d_model = x_prompt.shape
    n_s, dec, _ = x_sample.shape
    n_pages = page_table.shape[1]
    past_len = n_pages * PAGE_SIZE
    wb = state_nsa_win_k.shape[2]
    assert w_in.shape[0] == 1, "one layer only"
    assert seq % 256 == 0 and seq // CMP_STRIDE == LANES and seq >= WINDOW
    assert dec == 8 and wb == WINDOW and past_len % SEL_BLOCK == 0 and past_len // CMP_STRIDE == LANES
    assert ((past_len + dec) // CMP_STRIDE) * CMP_STRIDE <= past_len
    lam_init = 0.8 - 0.6 * math.exp(-0.3 * 0)
    col_gb, col_gates, _ = _proj_layout(d_model)

    w_packed = _pack_w_in(w_in[0], d_model)
    w1_kv = jnp.stack([cmp_k_w1[0], cmp_v_w1[0]]).astype(BF16)
    w1b = w1_kv.reshape(2, 2, CMP_STRIDE, HEAD_DIM, CMP_HIDDEN)
    w1ab_kv = jnp.concatenate([w1b[:, 0], w1b[:, 1]], axis=-1)
    pe_kv = jnp.stack([cmp_k_pe[0], cmp_v_pe[0]]).reshape(2, CMP_BLOCK * HEAD_DIM)
    w2_kv = jnp.stack([cmp_k_w2[0], cmp_v_w2[0]]).astype(BF16)
    lams = tuple(a[0].reshape(1, HEAD_DIM) for a in (diff_lq1, diff_lk1, diff_lq2, diff_lk2))
    diff_g = diff_norm_g[0].reshape(1, DIFF_VDIM)
    w_nsa = w_proj_nsa[0].astype(BF16)
    w_diff = w_proj_diff[0].astype(BF16)
    w_o = w_out[0].astype(BF16)
    page_flat = page_table.reshape(-1).astype(jnp.int32)
    pos_kv = _pos_terms(pe_kv, w1_kv)

    def split_rows(proj, n, s):
        kv0 = COL_KV
        seg = lambda i: proj[:, kv0 + i * NSA_KV_W:kv0 + (i + 1) * NSA_KV_W].reshape(1, n, s, NSA_GROUPS, HEAD_DIM)
        dk = proj[:, COL_DK:COL_DK + DIFF_W].reshape(1, n, s, DIFF_HEADS, DIFF_VDIM)
        dv = proj[:, COL_DV:COL_DV + DIFF_W].reshape(1, n, s, DIFF_HEADS, DIFF_VDIM)
        return [seg(i) for i in range(6)], dk, dv

    xp = x_prompt.reshape(n_p * seq, d_model)
    cos_p, sin_p = _rope_tables(jnp.arange(seq))
    proj_p = _project(_rmsnorm(xp, norm_g[0], BF16), w_packed,
                      jnp.tile(cos_p, (n_p, 1)), jnp.tile(sin_p, (n_p, 1)))
    kvc = _compress_prompt(proj_p, w1ab_kv, pos_kv, w2_kv, n_p, seq)
    o_nsa_p = _nsa_prompt(proj_p, kvc, col_gates, n_p, seq)
    o_diff_p = _diff_prompt(proj_p, lams, diff_g, lam_init, n_p, seq)
    merged_p = _mix1(o_nsa_p, w_nsa, o_diff_p, w_diff, proj_p, col_gb)
    y_prompt = _mix2(merged_p, w_o, xp, final_norm_g).reshape(n_p, seq, d_model)
    segs_p, dk_p, dv_p = split_rows(proj_p, n_p, seq)
    keep_p = min(WINDOW, seq)
    segs_p[4] = segs_p[4][:, :, seq - keep_p:]
    segs_p[5] = segs_p[5][:, :, seq - keep_p:]

    xs = x_sample.reshape(n_s * dec, d_model)
    cos_s, sin_s = _rope_tables(past_len + jnp.arange(dec))
    proj_s = _project(_rmsnorm(xs, norm_g[0], BF16), w_packed,
                      jnp.tile(cos_s, (n_s, 1)), jnp.tile(sin_s, (n_s, 1)))
    fs_k = _cmp_partials(cache_nsa_cmp_k[0], page_flat, w1ab_kv[0], n_s, n_pages)
    fs_v = _cmp_partials(cache_nsa_cmp_v[0], page_flat, w1ab_kv[1], n_s, n_pages)
    o_cmp_s, sel_s = _cmp_sample(fs_k, fs_v, pos_kv, w2_kv, proj_s, n_s, dec, past_len)
    o_nsa_s, win_k_s, win_v_s = _nsa_sample(cache_nsa_sel_k[0], cache_nsa_sel_v[0], state_nsa_win_k[0],
                                            state_nsa_win_v[0], page_flat, proj_s, o_cmp_s, sel_s, col_gates,
                                            n_s, dec, n_pages)
    o_diff_s = _diff_sample(cache_diff_k[0], cache_diff_v[0], page_flat, proj_s, lams, diff_g, lam_init,
                            n_s, dec, n_pages)
    merged_s = _mix1(o_nsa_s, w_nsa, o_diff_s, w_diff, proj_s, col_gb)
    y_sample = _mix2(merged_s, w_o, xs, final_norm_g).reshape(n_s, dec, d_model)
    segs_s, dk_s, dv_s = split_rows(proj_s, n_s, dec)
    segs_s[4] = win_k_s.reshape(1, n_s, wb, NSA_GROUPS, HEAD_DIM)
    segs_s[5] = win_v_s.reshape(1, n_s, wb, NSA_GROUPS, HEAD_DIM)

    return (y_prompt, y_sample, *segs_p, dk_p, dv_p, *segs_s, dk_s, dv_s)
```

```python
import functools
import math

import jax
import jax.numpy as jnp
from jax import lax
from jax.experimental import pallas as pl
from jax.experimental.pallas import tpu as pltpu

F32 = jnp.float32
BF16 = jnp.bfloat16

HEAD_DIM = 128
NSA_GROUPS = 4
NSA_REP = 4
NSA_HEADS = NSA_GROUPS * NSA_REP
CMP_BLOCK = 32
CMP_STRIDE = 16
CMP_HIDDEN = 2 * HEAD_DIM
SEL_BLOCK = 64
N_SELECT = 16
WINDOW = 512
DIFF_HEADS = 8
DIFF_VDIM = 2 * HEAD_DIM
ROPE_THETA = 10000.0
NORM_EPS = 1e-6
FORCE_SCORE = 1e4
NEG_INF = -1e30
SCALE = HEAD_DIM ** -0.5
PAGE_SIZE = 128

LANES = 128
NSA_Q_W = NSA_HEADS * HEAD_DIM
NSA_KV_W = NSA_GROUPS * HEAD_DIM
DIFF_W = DIFF_HEADS * DIFF_VDIM
PAD_ROWS = 128

COL_Q = 0
COL_Z_NSA = COL_Q + NSA_Q_W
COL_DQ = COL_Z_NSA + NSA_Q_W
COL_DK = COL_DQ + DIFF_W
COL_DV = COL_DK + DIFF_W
COL_Z_DIFF = COL_DV + DIFF_W
COL_KV = COL_Z_DIFF + DIFF_W
COL_GA = COL_KV + 6 * NSA_KV_W
PROJ_TN = 512
VMEM_LIMIT = 56 * 1024 * 1024


def _sigmoid(x):
    return 1.0 / (1.0 + jnp.exp(-x))


def _dot(a, b):
    return jnp.dot(a, b, preferred_element_type=F32)


def _dot_nt(a, b):
    return lax.dot_general(a, b, (((1,), (1,)), ((), ())), preferred_element_type=F32)


def _params(*sem):
    return pltpu.CompilerParams(dimension_semantics=sem, vmem_limit_bytes=VMEM_LIMIT)


def _rmsnorm_kernel(x_ref, g_ref, o_ref):
    x = x_ref[...]
    ms = jnp.mean(x * x, axis=-1, keepdims=True)
    o_ref[...] = (x * lax.rsqrt(ms + NORM_EPS) * g_ref[...]).astype(o_ref.dtype)


def _rmsnorm(x2d, g, out_dtype):
    m, d = x2d.shape
    tm = min(256, m)
    return pl.pallas_call(
        _rmsnorm_kernel,
        grid=(m // tm,),
        in_specs=[pl.BlockSpec((tm, d), lambda i: (i, 0)), pl.BlockSpec((1, d), lambda i: (0, 0))],
        out_specs=pl.BlockSpec((tm, d), lambda i: (i, 0)),
        out_shape=jax.ShapeDtypeStruct((m, d), out_dtype),
        compiler_params=_params("parallel"),
        name="rmsnorm",
    )(x2d, g.reshape(1, d))


def _proj_layout(d_model):
    col_gb = COL_GA + d_model
    col_gates = col_gb + d_model
    n_cols = col_gates + PROJ_TN
    return col_gb, col_gates, n_cols


def _pack_w_in(w, d_model):
    splits = (NSA_Q_W, 6 * NSA_KV_W, 3 * NSA_HEADS, NSA_Q_W, DIFF_W, DIFF_W, DIFF_W, DIFF_W, d_model, d_model)
    cuts = [sum(splits[:i + 1]) for i in range(len(splits) - 1)]
    q, kv, gates, z_nsa, dq, dk, dv, z_diff, g_a, g_b = jnp.split(w.T, cuts, axis=0)
    gates = gates.reshape(NSA_GROUPS, NSA_REP * 3, w.shape[0])
    gates = jnp.pad(gates, ((0, 0), (0, LANES - NSA_REP * 3), (0, 0))).reshape(NSA_GROUPS * LANES, w.shape[0])
    return jnp.concatenate([q, z_nsa, dq, dk, dv, z_diff, kv, g_a, g_b, gates], axis=0).astype(BF16)


def _proj_kernel(h_ref, w_ref, cos_ref, sin_ref, o_ref, kvn_ref, *, rope_blocks, silu_blocks, sig_start, kv_blocks,
                 row_chunk):
    j = pl.program_id(1)

    def _in(blocks):
        pred = None
        for lo, hi in blocks:
            c = (j >= lo) & (j < hi)
            pred = c if pred is None else (pred | c)
        return pred

    is_rope = _in(rope_blocks)
    is_silu = _in(silu_blocks)
    is_sig = j >= sig_start
    w = w_ref[...]
    for r0 in range(0, o_ref.shape[0], row_chunk):
        rows = slice(r0, r0 + row_chunk)
        x = _dot_nt(h_ref[rows, :], w)
        a = jnp.where(is_rope, cos_ref[rows, :], 1.0)
        b = jnp.where(is_rope, sin_ref[rows, :], 0.0)
        lin = jnp.concatenate(
            [x[:, c * HEAD_DIM:(c + 1) * HEAD_DIM] * a
             + pltpu.roll(x[:, c * HEAD_DIM:(c + 1) * HEAD_DIM], HEAD_DIM // 2, 1) * b
             for c in range(x.shape[1] // HEAD_DIM)], axis=1)
        sg = _sigmoid(x)
        o_ref[rows, :] = jnp.where(is_sig, sg, jnp.where(is_silu, x * sg, lin))

    @pl.when((j >= kv_blocks[0]) & (j < kv_blocks[1]))
    def _():
        for g in range(NSA_GROUPS):
            kvn_ref[0, pl.ds(g, o_ref.shape[0], stride=NSA_GROUPS), :] = o_ref[:, g * HEAD_DIM:(g + 1) * HEAD_DIM]


def _project(h, w_packed, cos, sin):
    m, d = h.shape
    n_cols = w_packed.shape[0]
    tm = min(1024, m)
    tn = PROJ_TN
    b = lambda col: col // tn
    kv0 = b(COL_KV)
    rope_blocks = ((b(COL_Q), b(COL_Z_NSA)), (b(COL_DQ), b(COL_DV)),
                   (kv0, kv0 + 1), (kv0 + 2, kv0 + 3), (kv0 + 4, kv0 + 5))
    silu_blocks = ((b(COL_Z_NSA), b(COL_DQ)), (b(COL_Z_DIFF), b(COL_KV)))
    assert tn == NSA_KV_W
    kern = functools.partial(_proj_kernel, rope_blocks=rope_blocks, silu_blocks=silu_blocks, sig_start=b(COL_GA),
                             kv_blocks=(kv0, kv0 + 6), row_chunk=min(256, tm))
    return pl.pallas_call(
        kern,
        grid=(m // tm, n_cols // tn),
        in_specs=[pl.BlockSpec((tm, d), lambda i, j: (i, 0)),
                  pl.BlockSpec((tn, d), lambda i, j: (j, 0)),
                  pl.BlockSpec((tm, HEAD_DIM), lambda i, j: (i, 0)),
                  pl.BlockSpec((tm, HEAD_DIM), lambda i, j: (i, 0))],
        out_specs=[pl.BlockSpec((tm, tn), lambda i, j: (i, j)),
                   pl.BlockSpec((1, tm * NSA_GROUPS, HEAD_DIM), lambda i, j: (jnp.clip(j - kv0, 0, 5), i, 0))],
        out_shape=[jax.ShapeDtypeStruct((m, n_cols), F32),
                   jax.ShapeDtypeStruct((6, m * NSA_GROUPS, HEAD_DIM), F32)],
        compiler_params=_params("arbitrary", "arbitrary"),
        name="in_proj",
    )(h, w_packed, cos, sin)


def _rope_tables(pos):
    half = HEAD_DIM // 2
    inv = ROPE_THETA ** (-jnp.arange(half, dtype=F32) / half)
    ang = pos.astype(F32)[:, None] * inv[None, :]
    cos, sin = jnp.cos(ang), jnp.sin(ang)
    return jnp.concatenate([cos, cos], axis=1), jnp.concatenate([-sin, sin], axis=1)


def _pos_term_kernel(pe_ref, w1_ref, o_ref):
    pe = jnp.broadcast_to(pe_ref[0], (8, pe_ref.shape[2])).astype(BF16)
    o_ref[0] = _dot(pe, w1_ref[0])


def _pos_terms(pe_kv, w1_kv):
    kdim = pe_kv.shape[1]
    out = pl.pallas_call(
        _pos_term_kernel,
        grid=(2,),
        in_specs=[pl.BlockSpec((1, 1, kdim), lambda i: (i, 0, 0)),
                  pl.BlockSpec((1, kdim, CMP_HIDDEN), lambda i: (i, 0, 0))],
        out_specs=pl.BlockSpec((1, 8, CMP_HIDDEN), lambda i: (i, 0, 0)),
        out_shape=jax.ShapeDtypeStruct((2, 8, CMP_HIDDEN), F32),
        compiler_params=_params("arbitrary"),
        name="cmp_pos_term",
    )(pe_kv.reshape(2, 1, kdim), w1_kv)
    return out[:, 0:1, :]


def _finish_compress(first, second, pos, w2):
    rows = first.shape[0]
    shifted = pltpu.roll(second, rows - 1, 0)
    pre = first + shifted + pos
    hid = pre * _sigmoid(pre)
    return _dot(hid.astype(BF16), w2)


def _compress_prompt_kernel(x0_ref, x1_ref, x2_ref, x3_ref, w1_ref, pos_ref, w2_ref, o_ref, *, nh):
    x_refs = (x0_ref, x1_ref, x2_ref, x3_ref)
    acc = jnp.zeros((NSA_GROUPS * nh, 2 * CMP_HIDDEN), F32)
    for s in range(CMP_STRIDE):
        xs = jnp.concatenate([x[pl.ds(s, nh, stride=CMP_STRIDE), :] for x in x_refs], axis=0)
        acc = acc + _dot(xs.astype(BF16), w1_ref[0, s])
    out = _finish_compress(acc[:, :CMP_HIDDEN], acc[:, CMP_HIDDEN:], pos_ref[0], w2_ref[0])
    o_ref[0, 0] = out.reshape(NSA_GROUPS, nh, HEAD_DIM)


def _compress_prompt(proj, w1ab_kv, pos_kv, w2_kv, n, seq):
    nh = seq // CMP_STRIDE
    cb = COL_KV // HEAD_DIM
    kern = functools.partial(_compress_prompt_kernel, nh=nh)

    def group_spec(g):
        return pl.BlockSpec((seq, HEAD_DIM), lambda c, i: (i, cb + c * NSA_GROUPS + g))

    return pl.pallas_call(
        kern,
        grid=(2, n),
        in_specs=[group_spec(g) for g in range(NSA_GROUPS)]
        + [pl.BlockSpec((1, CMP_STRIDE, HEAD_DIM, 2 * CMP_HIDDEN), lambda c, i: (c, 0, 0, 0)),
           pl.BlockSpec((1, 1, CMP_HIDDEN), lambda c, i: (c, 0, 0)),
           pl.BlockSpec((1, CMP_HIDDEN, HEAD_DIM), lambda c, i: (c, 0, 0))],
        out_specs=pl.BlockSpec((1, 1, NSA_GROUPS, nh, HEAD_DIM), lambda c, i: (c, i, 0, 0, 0)),
        out_shape=jax.ShapeDtypeStruct((2, n, NSA_GROUPS, nh, HEAD_DIM), F32),
        compiler_params=_params("arbitrary", "arbitrary"),
        name="compress_prompt",
    )(proj, proj, proj, proj, w1ab_kv, pos_kv, w2_kv)


QK_SCALE = SCALE * math.log2(math.e)


def _scaled_q(q):
    return (q * QK_SCALE).astype(BF16)


def _cmp_probs(qs, kc, trow, nc):
    s = _dot_nt(qs, kc)
    cidx = lax.broadcasted_iota(jnp.int32, s.shape, 1)
    vmask = ((cidx * CMP_STRIDE + CMP_BLOCK) <= trow + 1) & (cidx < nc)
    sm = jnp.where(vmask, s, NEG_INF)
    e = jnp.exp2(sm - jnp.max(sm, axis=-1, keepdims=True))
    p = e / jnp.sum(e, axis=-1, keepdims=True)
    return jnp.where(vmask, p, 0.0)


def _block_scores(psum, nblk):
    nc_rows = psum.shape[1]
    cc = lax.broadcasted_iota(jnp.int32, (nc_rows, LANES), 0) * CMP_STRIDE
    jj = lax.broadcasted_iota(jnp.int32, (nc_rows, LANES), 1)
    cover = (cc < jj * SEL_BLOCK + SEL_BLOCK) & (cc + CMP_BLOCK > jj * SEL_BLOCK) & (jj < nblk)
    cover = jnp.where(cover, 1.0, 0.0).astype(BF16)
    hi = psum.astype(BF16)
    r1 = psum - hi.astype(F32)
    mid = r1.astype(BF16)
    lo = (r1 - mid.astype(F32)).astype(BF16)
    return _dot(hi, cover) + _dot(mid, cover) + _dot(lo, cover)


def _rank_select(ranked, nblk, nsel):
    rt = ranked.T[0:nblk]
    ji = lax.broadcasted_iota(jnp.int32, rt.shape, 0)
    rank = jnp.zeros(rt.shape, F32)
    for i in range(nblk):
        row = rt[i:i + 1, :]
        beats = (row > rt) | ((row == rt) & (ji > i))
        rank = rank + jnp.where(beats, 1.0, 0.0)
    sel_t = jnp.where(rank < nsel, 1.0, 0.0)
    sel_t = jnp.concatenate([sel_t, jnp.zeros((LANES - nblk, rt.shape[1]), F32)], axis=0)
    return sel_t.T


def _expand_blocks(sel01, n_keys):
    jj = lax.broadcasted_iota(jnp.int32, (LANES, n_keys), 0)
    kk = lax.broadcasted_iota(jnp.int32, (LANES, n_keys), 1)
    expand = jnp.where((kk // SEL_BLOCK) == jj, 1.0, 0.0).astype(BF16)
    return _dot(sel01, expand)


def _softmax_weights(s):
    e = jnp.exp2(s - jnp.max(s, axis=-1, keepdims=True))
    return e, jnp.sum(e, axis=-1, keepdims=True)


def _masked_attend(qs, k, v, mask):
    e, l = _softmax_weights(jnp.where(mask, _dot_nt(qs, k), NEG_INF))
    return _dot(e.astype(BF16), v) / l


def _biased_attend(qs, k, v, bias):
    e, l = _softmax_weights(_dot_nt(qs, k) + bias)
    return _dot(e.astype(BF16), v) / l


def _online_update(m, l, acc, s, v):
    m_new = jnp.maximum(m, jnp.max(s, axis=-1, keepdims=True))
    alpha = jnp.exp2(m - m_new)
    e = jnp.exp2(s - m_new)
    return m_new, alpha * l + jnp.sum(e, axis=-1, keepdims=True), alpha * acc + _dot(e.astype(BF16), v)


def _nsa_prompt_kernel(q_ref, kc_ref, vc_ref, ks_ref, vs_ref, kw_ref, vw_ref, gate_ref, z_ref, o_ref,
                       ksb, vsb, kwb, vwb, osel, *, tq, seq, nc, nblk, nsel, kstep):
    qb = pl.program_id(2)
    q0 = qb * tq

    @pl.when(qb == 0)
    def _():
        ksb[...] = ks_ref[...].astype(BF16)
        vsb[...] = vs_ref[...].astype(BF16)
        kwb[...] = kw_ref[...].astype(BF16)
        vwb[...] = vw_ref[...].astype(BF16)

    q = q_ref[...]
    qs = _scaled_q(jnp.concatenate([q[:, r * HEAD_DIM:(r + 1) * HEAD_DIM] for r in range(NSA_REP)], axis=0))

    kc = kc_ref[0, 0, 0].astype(BF16)
    vc = vc_ref[0, 0, 0].astype(BF16)
    ncp = kc.shape[0]
    trow = (lax.broadcasted_iota(jnp.int32, (NSA_REP * tq, ncp), 0) & (tq - 1)) + q0
    p = _cmp_probs(qs, kc, trow, nc)
    o_cmp = _dot(p.astype(BF16), vc)

    psum = p[0:tq] + p[tq:2 * tq] + p[2 * tq:3 * tq] + p[3 * tq:4 * tq]
    score = _block_scores(psum, nblk)
    jl = lax.broadcasted_iota(jnp.int32, (tq, LANES), 1)
    tl = lax.broadcasted_iota(jnp.int32, (tq, LANES), 0) + q0
    cur = tl // SEL_BLOCK
    forced = (jl == 0) | (jl == cur) | (jl == cur - 1)
    ranked = jnp.where(forced, FORCE_SCORE, jnp.where(jl <= cur, score, -1.0))
    sel01 = jnp.where(jl <= cur, _rank_select(ranked, nblk, nsel), 0.0).astype(BF16)

    for v in range(seq // kstep):
        ext = (v + 1) * kstep

        @pl.when(q0 // kstep == v)
        def _(ext=ext):
            key_sel = _expand_blocks(sel01, ext)
            kpos = lax.broadcasted_iota(jnp.int32, (tq, ext), 1)
            tpos = lax.broadcasted_iota(jnp.int32, (tq, ext), 0) + q0
            bias = jnp.where((key_sel > 0.5) & (kpos <= tpos), 0.0, NEG_INF)
            ks = ksb[0:ext, :]
            vs = vsb[0:ext, :]
            for r in range(NSA_REP):
                osel[r * tq:(r + 1) * tq, :] = _biased_attend(qs[r * tq:(r + 1) * tq], ks, vs, bias)

    span = min(tq + WINDOW, seq)
    start = pl.multiple_of(jnp.clip(q0 + tq - span, 0, seq - span), tq)
    wk = lax.broadcasted_iota(jnp.int32, (tq, span), 1) + start
    wd = (lax.broadcasted_iota(jnp.int32, (tq, span), 0) + q0) - wk
    win_bias = jnp.where((wd >= 0) & (wd < WINDOW), 0.0, NEG_INF)
    kw = kwb[pl.ds(start, span), :]
    vw = vwb[pl.ds(start, span), :]
    gt = gate_ref[...]
    z = z_ref[...]
    for r in range(NSA_REP):
        o_sel = osel[r * tq:(r + 1) * tq, :]
        o_win = _biased_attend(qs[r * tq:(r + 1) * tq], kw, vw, win_bias)
        o = (gt[:, 3 * r:3 * r + 1] * o_cmp[r * tq:(r + 1) * tq] + gt[:, 3 * r + 1:3 * r + 2] * o_sel
             + gt[:, 3 * r + 2:3 * r + 3] * o_win)
        sl = slice(r * HEAD_DIM, (r + 1) * HEAD_DIM)
        o_ref[:, sl] = (o * z[:, sl]).astype(o_ref.dtype)


def _nsa_prompt(proj, kvc, col_gates, n, seq):
    tq = 128
    nqb = seq // tq
    nh = kvc.shape[3]
    nblk = seq // SEL_BLOCK
    kern = functools.partial(_nsa_prompt_kernel, tq=tq, seq=seq, nc=nh - 1, nblk=nblk, nsel=min(N_SELECT, nblk),
                             kstep=256)
    gw = NSA_REP * HEAD_DIM
    kvb = COL_KV // HEAD_DIM

    def kv_spec(which):
        return pl.BlockSpec((seq, HEAD_DIM), lambda i, g, qb: (i, kvb + which * NSA_GROUPS + g))

    return pl.pallas_call(
        kern,
        grid=(n, NSA_GROUPS, nqb),
        in_specs=[pl.BlockSpec((tq, gw), lambda i, g, qb: (i * nqb + qb, COL_Q // gw + g)),
                  pl.BlockSpec((1, 1, 1, nh, HEAD_DIM), lambda i, g, qb: (0, i, g, 0, 0)),
                  pl.BlockSpec((1, 1, 1, nh, HEAD_DIM), lambda i, g, qb: (1, i, g, 0, 0)),
                  kv_spec(2), kv_spec(3), kv_spec(4), kv_spec(5),
                  pl.BlockSpec((tq, LANES), lambda i, g, qb: (i * nqb + qb, col_gates // LANES + g)),
                  pl.BlockSpec((tq, gw), lambda i, g, qb: (i * nqb + qb, COL_Z_NSA // gw + g))],
        out_specs=pl.BlockSpec((tq, gw), lambda i, g, qb: (i * nqb + qb, g)),
        out_shape=jax.ShapeDtypeStruct((n * seq, NSA_Q_W), BF16),
        scratch_shapes=[pltpu.VMEM((seq, HEAD_DIM), BF16)] * 4 + [pltpu.VMEM((NSA_REP * tq, HEAD_DIM), F32)],
        compiler_params=_params("arbitrary", "arbitrary", "arbitrary"),
        name="nsa_prompt",
    )(proj, kvc, kvc, proj, proj, proj, proj, proj, proj)


def _lambda(lq1_ref, lk1_ref, lq2_ref, lk2_ref, lam_init):
    a = jnp.sum(lq1_ref[...] * lk1_ref[...], axis=-1, keepdims=True)
    b = jnp.sum(lq2_ref[...] * lk2_ref[...], axis=-1, keepdims=True)
    return jnp.exp(a) - jnp.exp(b) + lam_init


def _diff_finish(o, g, z, lam_init):
    ms = jnp.mean(o * o, axis=-1, keepdims=True)
    return o * lax.rsqrt(ms + NORM_EPS) * g * (1.0 - lam_init) * z


def _diff_prompt_kernel(lq1_ref, lk1_ref, lq2_ref, lk2_ref, q_ref, k_ref, v_ref, z_ref, g_ref, o_ref,
                        kb, vb, *, tq, seq, lam_init, kstep):
    qb = pl.program_id(2)
    q0 = qb * tq

    @pl.when(qb == 0)
    def _():
        kb[...] = k_ref[...].astype(BF16)
        vb[...] = v_ref[...].astype(BF16)

    lam = _lambda(lq1_ref, lk1_ref, lq2_ref, lk2_ref, lam_init)
    q = _scaled_q(q_ref[...])

    for v in range(seq // kstep):
        ext = (v + 1) * kstep

        @pl.when(q0 // kstep == v)
        def _(ext=ext):
            causal = (lax.broadcasted_iota(jnp.int32, (tq, ext), 1)
                      <= lax.broadcasted_iota(jnp.int32, (tq, ext), 0) + q0)
            bias = jnp.where(causal, 0.0, NEG_INF)
            k = kb[0:ext, :]
            maps = []
            for c in range(2):
                sl = slice(c * HEAD_DIM, (c + 1) * HEAD_DIM)
                e, l = _softmax_weights(_dot_nt(q[:, sl], k[:, sl]) + bias)
                maps.append(e / l)
            w = maps[0] - lam * maps[1]
            o = _dot(w.astype(BF16), vb[0:ext, :])
            o_ref[...] = _diff_finish(o, g_ref[...], z_ref[...], lam_init).astype(o_ref.dtype)


def _diff_prompt(proj, lams, diff_g, lam_init, n, seq):
    tq = 256
    nqb = seq // tq
    kern = functools.partial(_diff_prompt_kernel, tq=tq, seq=seq, lam_init=lam_init, kstep=256)
    hw = DIFF_VDIM
    lam_spec = pl.BlockSpec((1, HEAD_DIM), lambda i, h, qb: (0, 0))
    return pl.pallas_call(
        kern,
        grid=(n, DIFF_HEADS, nqb),
        in_specs=[lam_spec, lam_spec, lam_spec, lam_spec,
                  pl.BlockSpec((tq, hw), lambda i, h, qb: (i * nqb + qb, COL_DQ // hw + h)),
                  pl.BlockSpec((seq, hw), lambda i, h, qb: (i, COL_DK // hw + h)),
                  pl.BlockSpec((seq, hw), lambda i, h, qb: (i, COL_DV // hw + h)),
                  pl.BlockSpec((tq, hw), lambda i, h, qb: (i * nqb + qb, COL_Z_DIFF // hw + h)),
                  pl.BlockSpec((1, hw), lambda i, h, qb: (0, 0))],
        out_specs=pl.BlockSpec((tq, hw), lambda i, h, qb: (i * nqb + qb, h)),
        out_shape=jax.ShapeDtypeStruct((n * seq, DIFF_W), BF16),
        scratch_shapes=[pltpu.VMEM((seq, hw), BF16)] * 2,
        compiler_params=_params("arbitrary", "arbitrary", "arbitrary"),
        name="diff_prompt",
    )(*lams, proj, proj, proj, proj, diff_g)


def _mix1_kernel(a_ref, wa_ref, b_ref, wb_ref, ga_ref, gb_ref, o_ref):
    pa = _dot(a_ref[...].astype(BF16), wa_ref[...].astype(BF16))
    pb = _dot(b_ref[...].astype(BF16), wb_ref[...].astype(BF16))
    o_ref[...] = (ga_ref[...] * pa + gb_ref[...] * pb).astype(o_ref.dtype)


def _mix1(o_nsa, w_nsa, o_diff, w_diff, proj, col_gb):
    m = o_nsa.shape[0]
    d_model = w_nsa.shape[1]
    tm = min(1024, m)
    tn = 512
    return pl.pallas_call(
        _mix1_kernel,
        grid=(m // tm, d_model // tn),
        in_specs=[pl.BlockSpec((tm, NSA_Q_W), lambda i, j: (i, 0)),
                  pl.BlockSpec((NSA_Q_W, tn), lambda i, j: (0, j)),
                  pl.BlockSpec((tm, DIFF_W), lambda i, j: (i, 0)),
                  pl.BlockSpec((DIFF_W, tn), lambda i, j: (0, j)),
                  pl.BlockSpec((tm, tn), lambda i, j: (i, COL_GA // tn + j)),
                  pl.BlockSpec((tm, tn), lambda i, j: (i, col_gb // tn + j))],
        out_specs=pl.BlockSpec((tm, tn), lambda i, j: (i, j)),
        out_shape=jax.ShapeDtypeStruct((m, d_model), BF16),
        compiler_params=_params("parallel", "arbitrary"),
        name="mix_gated_proj",
    )(o_nsa, w_nsa, o_diff, w_diff, proj, proj)


def _mix2_kernel(m_ref, w_ref, x_ref, g_ref, o_ref, *, n_chunk):
    k = pl.program_id(1)
    d_model = o_ref.shape[1]
    cols = [slice(n0, n0 + n_chunk) for n0 in range(0, d_model, n_chunk)]

    @pl.when(k == 0)
    def _():
        o_ref[...] = x_ref[...]

    mm = m_ref[...]
    for sl in cols:
        o_ref[:, sl] += _dot(mm, w_ref[:, sl])

    @pl.when(k == pl.num_programs(1) - 1)
    def _():
        ss = jnp.zeros((o_ref.shape[0], 1), F32)
        for sl in cols:
            y = o_ref[:, sl]
            ss = ss + jnp.sum(y * y, axis=-1, keepdims=True)
        r = lax.rsqrt(ss * (1.0 / d_model) + NORM_EPS)
        for sl in cols:
            o_ref[:, sl] = o_ref[:, sl] * r * g_ref[:, sl]


def _mix2(merged, w_out, x2d, final_g):
    m, d_model = x2d.shape
    tm = min(512, m)
    tk = 512
    return pl.pallas_call(
        functools.partial(_mix2_kernel, n_chunk=1024),
        grid=(m // tm, d_model // tk),
        in_specs=[pl.BlockSpec((tm, tk), lambda i, k: (i, k)),
                  pl.BlockSpec((tk, d_model), lambda i, k: (k, 0)),
                  pl.BlockSpec((tm, d_model), lambda i, k: (i, 0)),
                  pl.BlockSpec((1, d_model), lambda i, k: (0, 0))],
        out_specs=pl.BlockSpec((tm, d_model), lambda i, k: (i, 0)),
        out_shape=jax.ShapeDtypeStruct((m, d_model), F32),
        compiler_params=_params("parallel", "arbitrary"),
        name="out_proj_residual_norm",
    )(merged, w_out, x2d, final_g.reshape(1, d_model))


def _cmp_partial_kernel(pt_ref, *refs, nbs):
    del pt_ref
    x_refs, w_ref, o_ref = refs[:nbs], refs[nbs], refs[nbs + 1]
    hpp = PAGE_SIZE // CMP_STRIDE
    acc = jnp.zeros((nbs * NSA_GROUPS * hpp, 2 * CMP_HIDDEN), F32)

    def token_rows(s):
        return jnp.concatenate([x[pl.ds(NSA_GROUPS * s + g, hpp, stride=SLAB_PITCH), :]
                                for x in flat for g in range(NSA_GROUPS)], axis=0)

    flat = [x.reshape(hpp * SLAB_PITCH, HEAD_DIM) for x in x_refs]

    for s in range(0, CMP_STRIDE, 2):
        xs = jnp.concatenate([token_rows(s), token_rows(s + 1)], axis=1).astype(BF16)
        acc = acc + _dot(xs, w_ref[s // 2])
    o_ref[...] = acc.reshape(nbs, NSA_GROUPS, hpp, 2 * CMP_HIDDEN)


SLAB_ROWS = CMP_STRIDE * NSA_GROUPS
SLAB_PITCH = SLAB_ROWS + 8


def _cmp_partials(cache, page_flat, w1ab, n, n_pages):
    pool = cache.shape[0]
    hpp = PAGE_SIZE // CMP_STRIDE
    view = cache.reshape(pool, hpp, SLAB_ROWS, HEAD_DIM)
    w_pairs = w1ab.reshape(CMP_STRIDE // 2, 2 * HEAD_DIM, 2 * CMP_HIDDEN)
    nbs = min(16, n)
    kern = functools.partial(_cmp_partial_kernel, nbs=nbs)

    def page_spec(i):
        return pl.BlockSpec((1, hpp, SLAB_PITCH, HEAD_DIM),
                            lambda nb, p, pt: (pt[(nb * nbs + i) * n_pages + p], 0, 0, 0))

    return pl.pallas_call(
        kern,
        grid_spec=pltpu.PrefetchScalarGridSpec(
            num_scalar_prefetch=1,
            grid=(n // nbs, n_pages),
            in_specs=[page_spec(i) for i in range(nbs)]
            + [pl.BlockSpec((CMP_STRIDE // 2, 2 * HEAD_DIM, 2 * CMP_HIDDEN), lambda nb, p, pt: (0, 0, 0))],
            out_specs=pl.BlockSpec((nbs, NSA_GROUPS, hpp, 2 * CMP_HIDDEN), lambda nb, p, pt: (nb, 0, p, 0)),
        ),
        out_shape=jax.ShapeDtypeStruct((n, NSA_GROUPS, n_pages * hpp, 2 * CMP_HIDDEN), F32),
        compiler_params=_params("parallel", "arbitrary"),
        name="cmp_partials_sample",
    )(page_flat, *([view] * nbs), w_pairs)


def _cmp_sample_kernel(fsk_ref, fsv_ref, pos_ref, w2_ref, q_ref, ocmp_ref, sel_ref,
                       *, nb2, nh, nc, past_len, dec, nblk, nsel):
    rows = nb2 * NSA_GROUPS * nh

    def compress(fs_ref, c):
        fs = fs_ref[...].reshape(rows, 2 * CMP_HIDDEN)
        return _finish_compress(fs[:, :CMP_HIDDEN], fs[:, CMP_HIDDEN:], pos_ref[c], w2_ref[c])

    kc = compress(fsk_ref, 0).astype(BF16)
    vc = compress(fsv_ref, 1).astype(BF16)
    trow = (lax.broadcasted_iota(jnp.int32, (NSA_REP * dec, nh), 0) & (dec - 1)) + past_len
    scores = []
    for i in range(nb2):
        qi = q_ref[i * dec:(i + 1) * dec, :]
        for g in range(NSA_GROUPS):
            base = (i * NSA_GROUPS + g) * nh
            c0 = g * NSA_REP * HEAD_DIM
            qg = _scaled_q(jnp.concatenate([qi[:, c0 + r * HEAD_DIM:c0 + (r + 1) * HEAD_DIM] for r in range(NSA_REP)],
                                           axis=0))
            p = _cmp_probs(qg, kc[base:base + nh], trow, nc)
            o = _dot(p.astype(BF16), vc[base:base + nh])
            for r in range(NSA_REP):
                ocmp_ref[i * dec:(i + 1) * dec, c0 + r * HEAD_DIM:c0 + (r + 1) * HEAD_DIM] = o[r * dec:(r + 1) * dec]
            psum = p[0:dec] + p[dec:2 * dec] + p[2 * dec:3 * dec] + p[3 * dec:4 * dec]
            scores.append(_block_scores(psum, nblk))
    score = jnp.concatenate(scores, axis=0)
    jl = lax.broadcasted_iota(jnp.int32, score.shape, 1)
    ranked = jnp.where((jl == 0) | (jl == nblk - 1), FORCE_SCORE, score)
    sel_ref[...] = _rank_select(ranked, nblk, nsel).reshape(nb2, NSA_GROUPS, dec, LANES)


def _cmp_sample(fs_k, fs_v, pos_kv, w2_kv, proj_s, n, dec, past_len):
    nh = fs_k.shape[2]
    nb2 = min(4, n)
    nblk = past_len // SEL_BLOCK
    kern = functools.partial(_cmp_sample_kernel, nb2=nb2, nh=nh, nc=nh - 1, past_len=past_len, dec=dec,
                             nblk=nblk, nsel=min(N_SELECT - 1, nblk))
    fs_spec = pl.BlockSpec((nb2, NSA_GROUPS, nh, 2 * CMP_HIDDEN), lambda i: (i, 0, 0, 0))
    return pl.pallas_call(
        kern,
        grid=(n // nb2,),
        in_specs=[fs_spec, fs_spec,
                  pl.BlockSpec((2, 1, CMP_HIDDEN), lambda i: (0, 0, 0)),
                  pl.BlockSpec((2, CMP_HIDDEN, HEAD_DIM), lambda i: (0, 0, 0)),
                  pl.BlockSpec((nb2 * dec, NSA_Q_W), lambda i: (i, 0))],
        out_specs=[pl.BlockSpec((nb2 * dec, NSA_Q_W), lambda i: (i, 0)),
                   pl.BlockSpec((nb2, NSA_GROUPS, dec, LANES), lambda i: (i, 0, 0, 0))],
        out_shape=[jax.ShapeDtypeStruct((n * dec, NSA_Q_W), F32),
                   jax.ShapeDtypeStruct((n, NSA_GROUPS, dec, LANES), F32)],
        compiler_params=_params("parallel"),
        name="cmp_attend_select_sample",
    )(fs_k, fs_v, pos_kv, w2_kv, proj_s)


def _pad_new(x):
    return jnp.concatenate([x, jnp.zeros((PAD_ROWS - x.shape[0], x.shape[1]), x.dtype)], axis=0).astype(BF16)


def _nsa_sample_kernel(pt_ref, *refs, n_pages, past_len, dec, wb):
    del pt_ref
    ksel_pages = refs[:n_pages]
    vsel_pages = refs[n_pages:2 * n_pages]
    (bufk_ref, bufv_ref, q_ref, ksn_ref, vsn_ref, kwn_ref, vwn_ref, ocmp_ref, gate_ref, z_ref, sel_ref,
     o_ref, nbk_ref, nbv_ref) = refs[2 * n_pages:]
    rows = NSA_REP * dec
    t_of_row = lambda shape: lax.broadcasted_iota(jnp.int32, shape, 0) & (dec - 1)

    def group_rows(ref, g, n_tok):
        return ref[0, pl.ds(g, n_tok, stride=NSA_GROUPS), :]

    keep_rows = (wb - dec) * NSA_GROUPS
    for buf_ref, new_ref, out_ref in ((bufk_ref, kwn_ref, nbk_ref), (bufv_ref, vwn_ref, nbv_ref)):
        out_ref[0, 0:keep_rows, :] = buf_ref[0, dec * NSA_GROUPS:wb * NSA_GROUPS, :]
        for g in range(NSA_GROUPS):
            out_ref[0, pl.ds(keep_rows + g, dec, stride=NSA_GROUPS), :] = new_ref[:, g * HEAD_DIM:(g + 1) * HEAD_DIM]

    jn = lax.broadcasted_iota(jnp.int32, (rows, PAD_ROWS), 1)
    new_ok = jnp.where((jn < dec) & (jn <= t_of_row((rows, PAD_ROWS))), 1.0, 0.0)
    n_win = wb + PAD_ROWS
    wi = lax.broadcasted_iota(jnp.int32, (rows, n_win), 1)
    wkpos = jnp.where(wi < wb, past_len - wb + wi, past_len + wi - wb)
    wd = (past_len + t_of_row((rows, n_win))) - wkpos
    win_mask = (wd >= 0) & (wd < WINDOW) & (wi < wb + dec)

    q = q_ref[...]
    gt = gate_ref[...]
    z = z_ref[...]
    for g in range(NSA_GROUPS):
        sl = slice(g * HEAD_DIM, (g + 1) * HEAD_DIM)
        c0 = g * NSA_REP * HEAD_DIM
        qg = _scaled_q(jnp.concatenate([q[:, c0 + r * HEAD_DIM:c0 + (r + 1) * HEAD_DIM] for r in range(NSA_REP)],
                                       axis=0))
        kg = jnp.concatenate([group_rows(pg, g, PAGE_SIZE).astype(BF16) for pg in ksel_pages]
                             + [_pad_new(ksn_ref[:, sl])], axis=0)
        vg = jnp.concatenate([group_rows(pg, g, PAGE_SIZE).astype(BF16) for pg in vsel_pages]
                             + [_pad_new(vsn_ref[:, sl])], axis=0)
        key_sel = _expand_blocks(sel_ref[0, g].astype(BF16), past_len)
        ok = jnp.concatenate([jnp.concatenate([key_sel] * NSA_REP, axis=0), new_ok], axis=1)
        o_sel = _masked_attend(qg, kg, vg, ok > 0.5)
        kw = jnp.concatenate([group_rows(bufk_ref, g, wb).astype(BF16), _pad_new(kwn_ref[:, sl])], axis=0)
        vw = jnp.concatenate([group_rows(bufv_ref, g, wb).astype(BF16), _pad_new(vwn_ref[:, sl])], axis=0)
        o_win = _masked_attend(qg, kw, vw, win_mask)
        for r in range(NSA_REP):
            cs = slice(c0 + r * HEAD_DIM, c0 + (r + 1) * HEAD_DIM)
            rs = slice(r * dec, (r + 1) * dec)
            gi = g * LANES + 3 * r
            o = (gt[:, gi:gi + 1] * ocmp_ref[:, cs] + gt[:, gi + 1:gi + 2] * o_sel[rs]
                 + gt[:, gi + 2:gi + 3] * o_win[rs])
            o_ref[:, cs] = o * z[:, cs]


def _nsa_sample(cache_k, cache_v, buf_k, buf_v, page_flat, proj_s, o_cmp, sel, col_gates, n, dec, n_pages):
    pool = cache_k.shape[0]
    wb = buf_k.shape[1]
    past_len = n_pages * PAGE_SIZE
    kview = cache_k.reshape(pool, PAGE_SIZE * NSA_GROUPS, HEAD_DIM)
    vview = cache_v.reshape(pool, PAGE_SIZE * NSA_GROUPS, HEAD_DIM)
    bk = buf_k.reshape(n, wb * NSA_GROUPS, HEAD_DIM)
    bv = buf_v.reshape(n, wb * NSA_GROUPS, HEAD_DIM)
    kern = functools.partial(_nsa_sample_kernel, n_pages=n_pages, past_len=past_len, dec=dec, wb=wb)
    kvb = COL_KV // NSA_KV_W

    def page_spec(p):
        return pl.BlockSpec((1, PAGE_SIZE * NSA_GROUPS, HEAD_DIM), lambda i, pt: (pt[i * n_pages + p], 0, 0))

    def row_spec(width, col_block):
        return pl.BlockSpec((dec, width), lambda i, pt: (i, col_block))

    buf_spec = pl.BlockSpec((1, wb * NSA_GROUPS, HEAD_DIM), lambda i, pt: (i, 0, 0))
    in_specs = ([page_spec(p) for p in range(n_pages)] + [page_spec(p) for p in range(n_pages)]
                + [buf_spec, buf_spec,
                   row_spec(NSA_Q_W, COL_Q // NSA_Q_W),
                   row_spec(NSA_KV_W, kvb + 2), row_spec(NSA_KV_W, kvb + 3),
                   row_spec(NSA_KV_W, kvb + 4), row_spec(NSA_KV_W, kvb + 5),
                   row_spec(NSA_Q_W, 0),
                   row_spec(NSA_GROUPS * LANES, col_gates // (NSA_GROUPS * LANES)),
                   row_spec(NSA_Q_W, COL_Z_NSA // NSA_Q_W),
                   pl.BlockSpec((1, NSA_GROUPS, dec, LANES), lambda i, pt: (i, 0, 0, 0))])
    return pl.pallas_call(
        kern,
        grid_spec=pltpu.PrefetchScalarGridSpec(
            num_scalar_prefetch=1,
            grid=(n,),
            in_specs=in_specs,
            out_specs=[row_spec(NSA_Q_W, 0), buf_spec, buf_spec],
        ),
        out_shape=[jax.ShapeDtypeStruct((n * dec, NSA_Q_W), F32),
                   jax.ShapeDtypeStruct((n, wb * NSA_GROUPS, HEAD_DIM), F32),
                   jax.ShapeDtypeStruct((n, wb * NSA_GROUPS, HEAD_DIM), F32)],
        compiler_params=_params("parallel"),
        name="nsa_sample",
    )(page_flat, *([kview] * n_pages), *([vview] * n_pages), bk, bv,
      proj_s, proj_s, proj_s, proj_s, proj_s, o_cmp, proj_s, proj_s, sel)


def _diff_sample_kernel(pt_ref, *refs, ppc, dec, lam_init):
    del pt_ref
    k_pages = refs[:ppc]
    v_pages = refs[ppc:2 * ppc]
    (lq1_ref, lk1_ref, lq2_ref, lk2_ref, q_ref, kn_ref, vn_ref, z_ref, g_ref, o_ref,
     m_sc, l_sc, acc_sc) = refs[2 * ppc:]
    j = pl.program_id(1)
    last = pl.num_programs(1) - 1

    @pl.when(j == 0)
    def _():
        m_sc[...] = jnp.full(m_sc.shape, NEG_INF, F32)
        l_sc[...] = jnp.zeros(l_sc.shape, F32)
        acc_sc[...] = jnp.zeros(acc_sc.shape, F32)

    hh = 2 * DIFF_HEADS
    nrow = hh * dec
    half = DIFF_HEADS * dec
    slab = PAGE_SIZE * DIFF_HEADS

    def rows_hd(x, width, c):
        return jnp.concatenate([x[:, h * DIFF_VDIM + c * width:h * DIFF_VDIM + (c + 1) * width]
                                for h in range(DIFF_HEADS)], axis=0)

    q = q_ref[...]
    qmap = [_scaled_q(rows_hd(q, HEAD_DIM, c)) for c in range(2)]
    ri = lax.broadcasted_iota(jnp.int32, (half, slab), 0)
    li = lax.broadcasted_iota(jnp.int32, (half, slab), 1)
    past_ok = (li & (DIFF_HEADS - 1)) == (ri // dec)

    state = [(m_sc[c * half:(c + 1) * half, 0:1], l_sc[c * half:(c + 1) * half, 0:1],
              acc_sc[c * half:(c + 1) * half, :]) for c in range(2)]
    for kp, vp in zip(k_pages, v_pages):
        vv = jnp.concatenate([vp[0, :, 0:DIFF_HEADS, :].reshape(slab, HEAD_DIM),
                              vp[0, :, DIFF_HEADS:hh, :].reshape(slab, HEAD_DIM)], axis=1).astype(BF16)
        for c in range(2):
            xb = kp[0, :, c * DIFF_HEADS:(c + 1) * DIFF_HEADS, :].reshape(slab, HEAD_DIM).astype(BF16)
            s = jnp.where(past_ok, _dot_nt(qmap[c], xb), NEG_INF)
            state[c] = _online_update(*state[c], s, vv)

    rn = lax.broadcasted_iota(jnp.int32, (half, half), 0)
    ln = lax.broadcasted_iota(jnp.int32, (half, half), 1)
    gate = jnp.where(j == last, 0, -half)
    new_ok = ((ln // dec) == (rn // dec)) & ((ln & (dec - 1)) <= (rn & (dec - 1)) + gate)
    vn = rows_hd(vn_ref[...], DIFF_VDIM, 0).astype(BF16)
    kn = kn_ref[...]
    for c in range(2):
        s = jnp.where(new_ok, _dot_nt(qmap[c], rows_hd(kn, HEAD_DIM, c).astype(BF16)), NEG_INF)
        m, l, acc = _online_update(*state[c], s, vn)
        m_sc[c * half:(c + 1) * half, :] = jnp.broadcast_to(m, (half, LANES))
        l_sc[c * half:(c + 1) * half, :] = jnp.broadcast_to(l, (half, LANES))
        acc_sc[c * half:(c + 1) * half, :] = acc

    @pl.when(j == last)
    def _():
        lam = _lambda(lq1_ref, lk1_ref, lq2_ref, lk2_ref, lam_init)
        o = acc_sc[...] / l_sc[:, 0:1]
        for h in range(DIFF_HEADS):
            vsl = slice(h * DIFF_VDIM, (h + 1) * DIFF_VDIM)
            o1 = o[h * dec:(h + 1) * dec]
            o2 = o[half + h * dec:half + (h + 1) * dec]
            o_ref[:, vsl] = _diff_finish(o1 - lam * o2, g_ref[...], z_ref[:, vsl], lam_init)


def _diff_cache_view(cache):
    pool = cache.shape[0]
    v = cache.reshape(pool, PAGE_SIZE, DIFF_HEADS, 2, HEAD_DIM).transpose(0, 1, 3, 2, 4)
    return v.reshape(pool, PAGE_SIZE, 2 * DIFF_HEADS, HEAD_DIM)


def _diff_sample(cache_k, cache_v, page_flat, proj_s, lams, diff_g, lam_init, n, dec, n_pages):
    kview = _diff_cache_view(cache_k)
    vview = _diff_cache_view(cache_v)
    n_chunks = 2 if n_pages % 2 == 0 else 1
    ppc = n_pages // n_chunks
    rows = 2 * DIFF_HEADS * dec
    kern = functools.partial(_diff_sample_kernel, ppc=ppc, dec=dec, lam_init=lam_init)

    def page_spec(p):
        return pl.BlockSpec((1, PAGE_SIZE, 2 * DIFF_HEADS, HEAD_DIM),
                            lambda i, j, pt: (pt[i * n_pages + j * ppc + p], 0, 0, 0))

    def row_spec(col0):
        return pl.BlockSpec((dec, DIFF_W), lambda i, j, pt: (i, col0 // DIFF_W))

    lam_spec = pl.BlockSpec((1, HEAD_DIM), lambda i, j, pt: (0, 0))
    in_specs = ([page_spec(p) for p in range(ppc)] + [page_spec(p) for p in range(ppc)]
                + [lam_spec] * 4
                + [row_spec(COL_DQ), row_spec(COL_DK), row_spec(COL_DV), row_spec(COL_Z_DIFF),
                   pl.BlockSpec((1, DIFF_VDIM), lambda i, j, pt: (0, 0))])
    return pl.pallas_call(
        kern,
        grid_spec=pltpu.PrefetchScalarGridSpec(
            num_scalar_prefetch=1,
            grid=(n, n_chunks),
            in_specs=in_specs,
            out_specs=pl.BlockSpec((dec, DIFF_W), lambda i, j, pt: (i, 0)),
            scratch_shapes=[pltpu.VMEM((rows, LANES), F32),
                            pltpu.VMEM((rows, LANES), F32),
                            pltpu.VMEM((rows, DIFF_VDIM), F32)],
        ),
        out_shape=jax.ShapeDtypeStruct((n * dec, DIFF_W), F32),
        compiler_params=_params("parallel", "arbitrary"),
        name="diff_sample",
    )(page_flat, *([kview] * ppc), *([vview] * ppc), *lams, proj_s, proj_s, proj_s, proj_s, diff_g)


def kernel(x_prompt, x_sample, cache_nsa_cmp_k, cache_nsa_cmp_v, cache_nsa_sel_k, cache_nsa_sel_v,
           state_nsa_win_k, state_nsa_win_v, cache_diff_k, cache_diff_v, page_table,
           norm_g, w_in, cmp_k_w1, cmp_k_pe, cmp_k_w2, cmp_v_w1, cmp_v_pe, cmp_v_w2,
           diff_lq1, diff_lk1, diff_lq2, diff_lk2, diff_norm_g, w_proj_nsa, w_proj_diff, w_out,
           final_norm_g):
    n_p, seq, d_model = x_prompt.shape
    n_s, dec, _ = x_sample.shape
    n_pages = page_table.shape[1]
    past_len = n_pages * PAGE_SIZE
    wb = state_nsa_win_k.shape[2]
    assert w_in.shape[0] == 1, "one layer only"
    assert seq % 256 == 0 and seq // CMP_STRIDE == LANES and seq >= WINDOW
    assert dec == 8 and wb == WINDOW and past_len % SEL_BLOCK == 0 and past_len // CMP_STRIDE == LANES
    assert ((past_len + dec) // CMP_STRIDE) * CMP_STRIDE <= past_len
    lam_init = 0.8 - 0.6 * math.exp(-0.3 * 0)
    col_gb, col_gates, _ = _proj_layout(d_model)

    w_packed = _pack_w_in(w_in[0], d_model)
    w1_kv = jnp.stack([cmp_k_w1[0], cmp_v_w1[0]]).astype(BF16)
    w1b = w1_kv.reshape(2, 2, CMP_STRIDE, HEAD_DIM, CMP_HIDDEN)
    w1ab_kv = jnp.concatenate([w1b[:, 0], w1b[:, 1]], axis=-1)
    pe_kv = jnp.stack([cmp_k_pe[0], cmp_v_pe[0]]).reshape(2, CMP_BLOCK * HEAD_DIM)
    w2_kv = jnp.stack([cmp_k_w2[0], cmp_v_w2[0]]).astype(BF16)
    lams = tuple(a[0].reshape(1, HEAD_DIM) for a in (diff_lq1, diff_lk1, diff_lq2, diff_lk2))
    diff_g = diff_norm_g[0].reshape(1, DIFF_VDIM)
    w_nsa = w_proj_nsa[0]
    w_diff = w_proj_diff[0]
    w_o = w_out[0].astype(BF16)
    page_flat = page_table.reshape(-1).astype(jnp.int32)
    pos_kv = _pos_terms(pe_kv, w1_kv)

    def split_rows(proj, kvn, n, s):
        segs = [kvn[i].reshape(1, n, s, NSA_GROUPS, HEAD_DIM) for i in range(6)]
        dk = proj[:, COL_DK:COL_DK + DIFF_W].reshape(1, n, s, DIFF_HEADS, DIFF_VDIM)
        dv = proj[:, COL_DV:COL_DV + DIFF_W].reshape(1, n, s, DIFF_HEADS, DIFF_VDIM)
        return segs, dk, dv

    xp = x_prompt.reshape(n_p * seq, d_model)
    cos_p, sin_p = _rope_tables(jnp.arange(seq))
    proj_p, kvn_p = _project(_rmsnorm(xp, norm_g[0], BF16), w_packed,
                             jnp.tile(cos_p, (n_p, 1)), jnp.tile(sin_p, (n_p, 1)))
    kvc = _compress_prompt(proj_p, w1ab_kv, pos_kv, w2_kv, n_p, seq)
    o_nsa_p = _nsa_prompt(proj_p, kvc, col_gates, n_p, seq)
    o_diff_p = _diff_prompt(proj_p, lams, diff_g, lam_init, n_p, seq)
    merged_p = _mix1(o_nsa_p, w_nsa, o_diff_p, w_diff, proj_p, col_gb)
    y_prompt = _mix2(merged_p, w_o, xp, final_norm_g).reshape(n_p, seq, d_model)
    segs_p, dk_p, dv_p = split_rows(proj_p, kvn_p, n_p, seq)
    keep_p = min(WINDOW, seq)
    segs_p[4] = segs_p[4][:, :, seq - keep_p:]
    segs_p[5] = segs_p[5][:, :, seq - keep_p:]

    xs = x_sample.reshape(n_s * dec, d_model)
    cos_s, sin_s = _rope_tables(past_len + jnp.arange(dec))
    proj_s, kvn_s = _project(_rmsnorm(xs, norm_g[0], BF16), w_packed,
                             jnp.tile(cos_s, (n_s, 1)), jnp.tile(sin_s, (n_s, 1)))
    fs_k = _cmp_partials(cache_nsa_cmp_k[0], page_flat, w1ab_kv[0], n_s, n_pages)
    fs_v = _cmp_partials(cache_nsa_cmp_v[0], page_flat, w1ab_kv[1], n_s, n_pages)
    o_cmp_s, sel_s = _cmp_sample(fs_k, fs_v, pos_kv, w2_kv, proj_s, n_s, dec, past_len)
    o_nsa_s, win_k_s, win_v_s = _nsa_sample(cache_nsa_sel_k[0], cache_nsa_sel_v[0], state_nsa_win_k[0],
                                            state_nsa_win_v[0], page_flat, proj_s, o_cmp_s, sel_s, col_gates,
                                            n_s, dec, n_pages)
    o_diff_s = _diff_sample(cache_diff_k[0], cache_diff_v[0], page_flat, proj_s, lams, diff_g, lam_init,
                            n_s, dec, n_pages)
    merged_s = _mix1(o_nsa_s, w_nsa, o_diff_s, w_diff, proj_s, col_gb)
    y_sample = _mix2(merged_s, w_o, xs, final_norm_g).reshape(n_s, dec, d_model)
    segs_s, dk_s, dv_s = split_rows(proj_s, kvn_s, n_s, dec)
    segs_s[4] = win_k_s.reshape(1, n_s, wb, NSA_GROUPS, HEAD_DIM)
    segs_s[5] = win_v_s.reshape(1, n_s, wb, NSA_GROUPS, HEAD_DIM)

    return (y_prompt, y_sample, *segs_p, dk_p, dv_p, *segs_s, dk_s, dv_s)
```
